```python
import math
import numpy as np
import jax
import jax.numpy as jnp
from jax import lax

D_MODEL = 1024
BATCH = 16
SEQ = 2048
DEPTH = 2
DEC_BATCH = 32
DEC_SEQ = 4
PAST_LEN = 16384
PAGE_SIZE = 128

MIX = 256
N_BRANCH = 4
SB_HEADS = 4
SB_DIM = MIX // SB_HEADS
POOL_WINDOWS = (2, 4, 8, 16)
POOL_GROUPS = len(POOL_WINDOWS)
POOL_GDIM = MIX // POOL_GROUPS
POOL_STATE = max(POOL_WINDOWS) - 1
GM_GROUPS = 4
GM_GDIM = MIX // GM_GROUPS
GM_CHUNK = 128
DF_HEADS = 4
DF_QK = MIX // (2 * DF_HEADS)
DF_V = MIX // DF_HEADS
MEM_TOKENS = 256
CX_HEADS = 4
CX_DIM = 64
CX_WIDTH = CX_HEADS * CX_DIM
FFN_DIM = 2816
CONV_W = 3
Q_BLOCK = 128
EPS = 1e-6
NEG_INF = -1e30
IN_SPLITS = (MIX,) * 9 + (N_BRANCH * D_MODEL,)
P_IN = sum(IN_SPLITS)

kernel_name = 'hybrid_stickbreak_pool_gmlp_diffattn_decoder_step'


def rms_norm(x, g):
    xf = x.astype(jnp.float32)
    y = xf * lax.rsqrt(jnp.mean(xf * xf, axis=-1, keepdims=True) + EPS)
    return (y * g.astype(jnp.float32)).astype(x.dtype)


def project_mixers(h, w_in, g_qd, g_kd):
    B, T, _ = h.shape
    offs = [int(o) for o in np.cumsum(IN_SPLITS)[:-1]]
    qa, ka, va, xb, uc, vc, qd, kd, vd, gates = jnp.split(h @ w_in, offs, axis=-1)
    qd = rms_norm(qd.reshape(B, T, DF_HEADS, 2, DF_QK), g_qd)
    kd = rms_norm(kd.reshape(B, T, DF_HEADS, 2, DF_QK), g_kd).reshape(B, T, DF_HEADS, 2 * DF_QK)
    return (qa.reshape(B, T, SB_HEADS, SB_DIM), ka.reshape(B, T, SB_HEADS, SB_DIM),
            va.reshape(B, T, SB_HEADS, SB_DIM), xb, uc, vc, qd, kd,
            vd.reshape(B, T, DF_HEADS, DF_V), gates.reshape(B, T, N_BRANCH, D_MODEL))


def stick_breaking(q, k, v, q_pos):
    B, Tq = q.shape[:2]
    z = jnp.einsum('bthd,bshd->bhts', q.astype(jnp.float32), k.astype(jnp.float32)) * SB_DIM ** -0.5
    mask = jnp.arange(k.shape[1])[None, :] < q_pos[:, None]
    log_keep = jnp.where(mask, jax.nn.log_sigmoid(-z), 0.0)
    between = lax.cumsum(log_keep, axis=3, reverse=True) - log_keep
    w = jnp.where(mask, jnp.exp(jax.nn.log_sigmoid(z) + between), 0.0)
    o = jnp.einsum('bhts,bshd->bthd', w.astype(v.dtype), v)
    return o.reshape(B, Tq, MIX)


def diff_attention(q, k, v, q_pos, lam):
    kk = k.reshape(k.shape[:3] + (2, DF_QK))
    s = jnp.einsum('bthcd,bshcd->cbhts', q.astype(jnp.float32), kk.astype(jnp.float32)) * DF_QK ** -0.5
    mask = jnp.arange(k.shape[1])[None, :] <= q_pos[:, None]
    p = jax.nn.softmax(jnp.where(mask, s, NEG_INF), axis=-1)
    a = p[0] - lam * p[1]
    return jnp.einsum('bhts,bshd->bthd', a.astype(v.dtype), v)


def sweep_query_blocks(fn, q, k, v, extra):
    outs = []
    for i in range(q.shape[1] // Q_BLOCK):
        lo, hi = i * Q_BLOCK, (i + 1) * Q_BLOCK
        outs.append(fn(q[:, lo:hi], k[:, :hi], v[:, :hi], jnp.arange(lo, hi), *extra))
    return jnp.concatenate(outs, axis=1)


def diff_lambda(lq1, lk1, lq2, lk2, lam_init):
    f = lambda a: a.astype(jnp.float32)
    return jnp.exp(jnp.sum(f(lq1) * f(lk1))) - jnp.exp(jnp.sum(f(lq2) * f(lk2))) + lam_init


def diff_output(o, g_subln, lam_init):
    B, T = o.shape[:2]
    return (rms_norm(o, g_subln) * (1.0 - lam_init)).reshape(B, T, MIX)


def multiscale_pool(xb_ext, n_prev, pos0, w_pool, s_pool):
    B, L, _ = xb_ext.shape
    T = L - n_prev
    xf = xb_ext.astype(jnp.float32)
    cs = jnp.concatenate([jnp.zeros_like(xf[:, :1]), jnp.cumsum(xf, axis=1)], axis=1)
    e = np.arange(n_prev, L)
    pos = pos0 + np.arange(T)
    groups = []
    for g, w in enumerate(POOL_WINDOWS):
        lo_c, hi_c = g * POOL_GDIM, (g + 1) * POOL_GDIM
        win_sum = cs[:, e + 1, lo_c:hi_c] - cs[:, np.maximum(e + 1 - w, 0), lo_c:hi_c]
        cnt = np.minimum(pos + 1, w).astype(np.float32)
        groups.append(win_sum / cnt[None, :, None] - xf[:, n_prev:, lo_c:hi_c])
    d = jnp.stack(groups, axis=2)
    y = jnp.einsum('btgc,gcd->btgd', d, w_pool.astype(jnp.float32)).reshape(B, T, MIX)
    return (y * s_pool.astype(jnp.float32)).astype(xb_ext.dtype)


def spatial_gating(u, v_n, w_s, b_s):
    B, T, _ = u.shape
    C = min(T, GM_CHUNK)
    ws = jnp.where(jnp.tril(jnp.ones((C, C), bool)), w_s[:, :C, :C], 0.0)
    vc = v_n.reshape(B, T // C, C, GM_GROUPS, GM_GDIM)
    mixed = jnp.einsum('gts,bnsgc->bntgc', ws, vc) + b_s[:, :C].T[None, None, :, :, None]
    return u * mixed.reshape(B, T, MIX).astype(u.dtype)


def merge_branches(outs, gates, w_branch, w_out):
    merged = sum(jax.nn.sigmoid(gates[:, :, i]) * (o @ w_branch[i]) for i, o in enumerate(outs))
    return merged @ w_out


def mem_kv(mem, w_ckv, g_ck):
    B, M, _ = mem.shape
    k, v = jnp.split(mem @ w_ckv, 2, axis=-1)
    return rms_norm(k.reshape(B, M, CX_HEADS, CX_DIM), g_ck), v.reshape(B, M, CX_HEADS, CX_DIM)


def cross_attend(h, mk, mv, w_cq, g_cq, w_co):
    B, T, _ = h.shape
    q = rms_norm((h @ w_cq).reshape(B, T, CX_HEADS, CX_DIM), g_cq)
    s = jnp.einsum('bthd,bmhd->bhtm', q.astype(jnp.float32), mk.astype(jnp.float32)) * CX_DIM ** -0.5
    p = jax.nn.softmax(s, axis=-1)
    o = jnp.einsum('bhtm,bmhd->bthd', p.astype(mv.dtype), mv)
    return o.reshape(B, T, CX_WIDTH) @ w_co


def conv_ffn(h, conv_prev, w_up, w_conv, b_conv, w_down):
    a, b = jnp.split(h @ w_up, 2, axis=-1)
    T = a.shape[1]
    ext = jnp.concatenate([conv_prev.astype(a.dtype), a], axis=1)
    a_c = b_conv + sum(w_conv[j] * ext[:, j:j + T] for j in range(CONV_W))
    return (jax.nn.silu(a_c) * b) @ w_down, ext[:, -(CONV_W - 1):]


def setup_inputs(seed: int = 0) -> dict:
    key = jax.random.key(seed)
    ks = iter(jax.random.split(key, 64))
    f32 = jnp.float32

    def normal(shape, scale=1.0):
        return jax.random.normal(next(ks), shape, f32) * scale

    def gain(shape):
        return 1.0 + 0.02 * normal(shape)

    n_pages = PAST_LEN // PAGE_SIZE
    n_used = DEC_BATCH * n_pages
    n_pool = n_used + n_used // 4
    page_table = jax.random.permutation(next(ks), n_pool)[:n_used].reshape(DEC_BATCH, n_pages).astype(jnp.int32)
    return {
        'x_prompt': normal((BATCH, SEQ, D_MODEL)),
        'x_sample': normal((DEC_BATCH, DEC_SEQ, D_MODEL)),
        'mem_prompt': normal((BATCH, MEM_TOKENS, D_MODEL)),
        'cache_sb_k': normal((DEPTH, n_pool, PAGE_SIZE, SB_HEADS, SB_DIM)),
        'cache_sb_v': normal((DEPTH, n_pool, PAGE_SIZE, SB_HEADS, SB_DIM)),
        'cache_df_k': normal((DEPTH, n_pool, PAGE_SIZE, DF_HEADS, 2 * DF_QK)),
        'cache_df_v': normal((DEPTH, n_pool, PAGE_SIZE, DF_HEADS, DF_V)),
        'cache_mem_k': normal((DEPTH, DEC_BATCH, MEM_TOKENS, CX_HEADS, CX_DIM)),
        'cache_mem_v': normal((DEPTH, DEC_BATCH, MEM_TOKENS, CX_HEADS, CX_DIM)),
        'state_pool': normal((DEPTH, DEC_BATCH, POOL_STATE, MIX)),
        'state_ffn_conv': normal((DEPTH, DEC_BATCH, CONV_W - 1, FFN_DIM)),
        'page_table': page_table,
        'g_mix': gain((DEPTH, D_MODEL)),
        'w_in': normal((DEPTH, D_MODEL, P_IN), D_MODEL ** -0.5),
        'w_branch': normal((DEPTH, N_BRANCH, MIX, D_MODEL), MIX ** -0.5),
        'w_out': normal((DEPTH, D_MODEL, D_MODEL), D_MODEL ** -0.5),
        'w_pool': normal((DEPTH, POOL_GROUPS, POOL_GDIM, POOL_GDIM), POOL_GDIM ** -0.5),
        's_pool': gain((DEPTH, MIX)),
        'g_gm': gain((DEPTH, MIX)),
        'w_spatial': normal((DEPTH, GM_GROUPS, GM_CHUNK, GM_CHUNK), GM_CHUNK ** -0.5),
        'b_spatial': gain((DEPTH, GM_GROUPS, GM_CHUNK)),
        'g_qd': gain((DEPTH, DF_QK)),
        'g_kd': gain((DEPTH, DF_QK)),
        'lam_q1': normal((DEPTH, DF_QK), 0.1),
        'lam_k1': normal((DEPTH, DF_QK), 0.1),
        'lam_q2': normal((DEPTH, DF_QK), 0.1),
        'lam_k2': normal((DEPTH, DF_QK), 0.1),
        'g_subln': gain((DEPTH, DF_V)),
        'g_cross': gain((DEPTH, D_MODEL)),
        'w_cq': normal((DEPTH, D_MODEL, CX_WIDTH), D_MODEL ** -0.5),
        'w_ckv': normal((DEPTH, D_MODEL, 2 * CX_WIDTH), D_MODEL ** -0.5),
        'w_co': normal((DEPTH, CX_WIDTH, D_MODEL), CX_WIDTH ** -0.5),
        'g_cq': gain((DEPTH, CX_DIM)),
        'g_ck': gain((DEPTH, CX_DIM)),
        'g_ffn': gain((DEPTH, D_MODEL)),
        'w_up': normal((DEPTH, D_MODEL, 2 * FFN_DIM), D_MODEL ** -0.5),
        'w_conv': normal((DEPTH, CONV_W, FFN_DIM), CONV_W ** -0.5),
        'b_conv': normal((DEPTH, FFN_DIM), 0.02),
        'w_down': normal((DEPTH, FFN_DIM, D_MODEL), FFN_DIM ** -0.5),
    }


def reference(x_prompt, x_sample, mem_prompt, cache_sb_k, cache_sb_v, cache_df_k, cache_df_v,
              cache_mem_k, cache_mem_v, state_pool, state_ffn_conv, page_table,
              g_mix, w_in, w_branch, w_out, w_pool, s_pool, g_gm, w_spatial, b_spatial,
              g_qd, g_kd, lam_q1, lam_k1, lam_q2, lam_k2, g_subln,
              g_cross, w_cq, w_ckv, w_co, g_cq, g_ck, g_ffn, w_up, w_conv, b_conv, w_down):
    B = x_prompt.shape[0]
    DB, TS, _ = x_sample.shape
    n_past = page_table.shape[1] * cache_sb_k.shape[2]

    def gather(cache_l):
        return cache_l[page_table].reshape((DB, n_past) + cache_l.shape[2:])

    xp, xs = x_prompt, x_sample
    sbk_p, sbv_p, dfk_p, dfv_p, mk_p, mv_p, pool_p, conv_p = [], [], [], [], [], [], [], []
    sbk_s, sbv_s, dfk_s, dfv_s, pool_s, conv_s, gmv_s = [], [], [], [], [], [], []
    q_pos_s = n_past + jnp.arange(TS)

    for l in range(DEPTH):
        lam_init = 0.8 - 0.6 * math.exp(-0.3 * l)
        lam = diff_lambda(lam_q1[l], lam_k1[l], lam_q2[l], lam_k2[l], lam_init)

        qa, ka, va, xb, uc, vc, qd, kd, vd, gates = project_mixers(rms_norm(xp, g_mix[l]), w_in[l], g_qd[l], g_kd[l])
        oa = sweep_query_blocks(stick_breaking, qa, ka, va, ())
        ob = multiscale_pool(xb, 0, 0, w_pool[l], s_pool[l])
        oc = spatial_gating(uc, rms_norm(vc, g_gm[l]), w_spatial[l], b_spatial[l])
        od = diff_output(sweep_query_blocks(diff_attention, qd, kd, vd, (lam,)), g_subln[l], lam_init)
        xp = xp + merge_branches((oa, ob, oc, od), gates, w_branch[l], w_out[l])
        sbk_p.append(ka)
        sbv_p.append(va)
        dfk_p.append(kd)
        dfv_p.append(vd)
        pool_p.append(xb[:, -POOL_STATE:])

        mk, mv = mem_kv(mem_prompt, w_ckv[l], g_ck[l])
        xp = xp + cross_attend(rms_norm(xp, g_cross[l]), mk, mv, w_cq[l], g_cq[l], w_co[l])
        mk_p.append(mk)
        mv_p.append(mv)

        zeros_prev = jnp.zeros((B, CONV_W - 1, FFN_DIM), xp.dtype)
        f, cst = conv_ffn(rms_norm(xp, g_ffn[l]), zeros_prev, w_up[l], w_conv[l], b_conv[l], w_down[l])
        xp = xp + f
        conv_p.append(cst)

        qa, ka, va, xb, uc, vc, qd, kd, vd, gates = project_mixers(rms_norm(xs, g_mix[l]), w_in[l], g_qd[l], g_kd[l])
        k_all = jnp.concatenate([gather(cache_sb_k[l]).astype(ka.dtype), ka], axis=1)
        v_all = jnp.concatenate([gather(cache_sb_v[l]).astype(va.dtype), va], axis=1)
        oa = stick_breaking(qa, k_all, v_all, q_pos_s)
        pool_ext = jnp.concatenate([state_pool[l].astype(xb.dtype), xb], axis=1)
        ob = multiscale_pool(pool_ext, POOL_STATE, n_past, w_pool[l], s_pool[l])
        vn = rms_norm(vc, g_gm[l])
        oc = spatial_gating(uc, vn, w_spatial[l], b_spatial[l])
        kd_all = jnp.concatenate([gather(cache_df_k[l]).astype(kd.dtype), kd], axis=1)
        vd_all = jnp.concatenate([gather(cache_df_v[l]).astype(vd.dtype), vd], axis=1)
        od = diff_output(diff_attention(qd, kd_all, vd_all, q_pos_s, lam), g_subln[l], lam_init)
        xs = xs + merge_branches((oa, ob, oc, od), gates, w_branch[l], w_out[l])
        sbk_s.append(ka)
        sbv_s.append(va)
        dfk_s.append(kd)
        dfv_s.append(vd)
        pool_s.append(pool_ext[:, -POOL_STATE:])
        gmv_s.append(vn)

        xs = xs + cross_attend(rms_norm(xs, g_cross[l]), cache_mem_k[l], cache_mem_v[l], w_cq[l], g_cq[l], w_co[l])

        f, cst = conv_ffn(rms_norm(xs, g_ffn[l]), state_ffn_conv[l], w_up[l], w_conv[l], b_conv[l], w_down[l])
        xs = xs + f
        conv_s.append(cst)

    st = jnp.stack
    return (xp, xs,
            st(sbk_p), st(sbv_p), st(dfk_p), st(dfv_p), st(mk_p), st(mv_p), st(pool_p), st(conv_p),
            st(sbk_s), st(sbv_s), st(dfk_s), st(dfv_s), st(pool_s), st(conv_s), st(gmv_s))
```

```python
import functools
import math

import numpy as np
import jax
import jax.numpy as jnp
from jax import lax
from jax.experimental import pallas as pl
from jax.experimental.pallas import tpu as pltpu

F32 = jnp.float32
BF16 = jnp.bfloat16

EPS = 1e-6
NEG_INF = -1e30
MIX = 256
HEAD = 64
DF_QK = 32
POOL_WINDOWS = (2, 4, 8, 16)
POOL_STATE = max(POOL_WINDOWS) - 1
POOL_PAD = 16
GM_CHUNK = 128
GM_GROUPS = 4
CONV_W = 3
LANES = 128
BF16_ROWS = 16
ROW_TILE = 512
ATT_BLOCK = 256
FFN_CHUNK = 256
PAGES_PER_STEP = 8
VMEM_LIMIT = 56 * 1024 * 1024

_NT = (((1,), (1,)), ((), ()))


def _params(sem):
    return pltpu.CompilerParams(dimension_semantics=sem, vmem_limit_bytes=VMEM_LIMIT)


def _rms(x, g):
    return x * lax.rsqrt(jnp.mean(x * x, axis=-1, keepdims=True) + EPS) * g


def _group_rms(v, g, bd):
    ms = jnp.dot((v * v).astype(BF16), bd, preferred_element_type=F32)
    return v * lax.rsqrt(ms + EPS) * g


def _block_diag_mean(width, group):
    i = np.arange(width)
    return jnp.asarray((i[:, None] // group == i[None, :] // group) / group, BF16)


def _full(shape):
    return pl.BlockSpec(shape, lambda *_: (0,) * len(shape))


def _lane_window():
    lane = lax.broadcasted_iota(jnp.int32, (1, MIX), 1)
    gdim = MIX // len(POOL_WINDOWS)
    wl = jnp.full((1, MIX), POOL_WINDOWS[-1], jnp.int32)
    for g in range(len(POOL_WINDOWS) - 2, -1, -1):
        wl = jnp.where(lane < (g + 1) * gdim, POOL_WINDOWS[g], wl)
    return wl


def _proj_in_body(x_ref, g_ref, wtok_ref, wkv_ref, gq_ref, gk_ref, ggm_ref, bd_ref,
                  qa_ref, xb_ref, uc_ref, vn_ref, qd_ref, ka_ref, va_ref, kd_ref, vd_ref, *, kv_major):
    h = _rms(x_ref[...], g_ref[...]).astype(BF16)
    y = jnp.dot(h, wtok_ref[...], preferred_element_type=F32)
    part = lambda i: y[:, i * MIX:(i + 1) * MIX]
    qa_ref[...] = part(0)
    xb_ref[...] = part(1)
    uc_ref[...] = part(2)
    vn_ref[...] = _rms(part(3), ggm_ref[...])
    qd_ref[...] = _group_rms(part(4), gq_ref[...], bd_ref[...])
    if kv_major:
        yt = lax.dot_general(wkv_ref[...], h, _NT, preferred_element_type=F32)
        kd = yt[2 * MIX:3 * MIX]
        ms = jnp.dot(bd_ref[...], (kd * kd).astype(BF16), preferred_element_type=F32)
        ka_ref[...] = yt[0:MIX]
        va_ref[...] = yt[MIX:2 * MIX]
        kd_ref[...] = kd * lax.rsqrt(ms + EPS) * gk_ref[...]
        vd_ref[...] = yt[3 * MIX:4 * MIX]
    else:
        y2 = jnp.dot(h, wkv_ref[...], preferred_element_type=F32)
        ka_ref[...] = y2[:, 0:MIX]
        va_ref[...] = y2[:, MIX:2 * MIX]
        kd_ref[...] = _group_rms(y2[:, 2 * MIX:3 * MIX], gk_ref[...], bd_ref[...])
        vd_ref[...] = y2[:, 3 * MIX:4 * MIX]


def _proj_in(x, g, wtok, wkv, gq, gk, ggm, bd32, n_seq, kv_major):
    T, D = x.shape
    S = T // n_seq
    tm = min(ROW_TILE, S)
    nt = S // tm
    row = pl.BlockSpec((tm, D), lambda i: (i, 0))
    out = pl.BlockSpec((tm, MIX), lambda i: (i, 0))
    tok_shape = jax.ShapeDtypeStruct((T, MIX), F32)
    if kv_major:
        kv_spec = pl.BlockSpec((None, MIX, tm), lambda i: (i // nt, 0, i % nt))
        kv_shape = jax.ShapeDtypeStruct((n_seq, MIX, S), F32)
    else:
        kv_spec, kv_shape = out, tok_shape
    return pl.pallas_call(
        functools.partial(_proj_in_body, kv_major=kv_major),
        grid=(T // tm,),
        in_specs=[row, _full((1, D)), _full(wtok.shape), _full(wkv.shape), _full((1, MIX)), _full(gk.shape),
                  _full((1, MIX)), _full((MIX, MIX))],
        out_specs=[out] * 5 + [kv_spec] * 4,
        out_shape=[tok_shape] * 5 + [kv_shape] * 4,
        compiler_params=_params(("parallel",)),
        name="proj_in",
    )(x, g, wtok, wkv, gq, gk, ggm, bd32)


def _local_prompt_body(xb_ref, uc_ref, vn_ref, wp_ref, sp_ref, ws_ref, bs_ref, ob_ref, oc_ref, ext_ref):
    S = xb_ref.shape[0]
    ext_ref[0:POOL_PAD, :] = jnp.zeros((POOL_PAD, MIX), F32)
    ext_ref[POOL_PAD:POOL_PAD + S, :] = xb_ref[...]
    wl = _lane_window()
    lane = lax.broadcasted_iota(jnp.int32, (1, MIX), 1)
    R = min(256, S)
    for c in range(S // R):
        x = xb_ref[c * R:(c + 1) * R, :]
        acc = x
        for k in range(1, POOL_STATE + 1):
            lo = POOL_PAD + c * R - k
            acc = acc + jnp.where(k < wl, ext_ref[lo:lo + R, :], 0.0)
        pos = c * R + lax.broadcasted_iota(jnp.int32, (R, 1), 0)
        cnt = jnp.minimum(pos + 1, wl).astype(F32)
        d = acc / cnt - x
        y = jnp.dot(d.astype(BF16), wp_ref[...], preferred_element_type=F32)
        ob_ref[c * R:(c + 1) * R, :] = y * sp_ref[...]

    C = min(GM_CHUNK, S)
    r = lax.broadcasted_iota(jnp.int32, (C, C), 0)
    cidx = lax.broadcasted_iota(jnp.int32, (C, C), 1)
    gdim = MIX // GM_GROUPS
    ws = [jnp.where(r >= cidx, ws_ref[g][:C, :C], 0.0).astype(BF16) for g in range(GM_GROUPS)]
    bias = bs_ref[0:C, :]
    for n in range(S // C):
        vn = vn_ref[n * C:(n + 1) * C, :].astype(BF16)
        mixed = jnp.dot(ws[0], vn, preferred_element_type=F32)
        for g in range(1, GM_GROUPS):
            mixed = jnp.where(lane >= g * gdim, jnp.dot(ws[g], vn, preferred_element_type=F32), mixed)
        oc_ref[n * C:(n + 1) * C, :] = uc_ref[n * C:(n + 1) * C, :] * (mixed + bias)


def _local_prompt(xb, uc, vn, wp_bd, sp, ws, bs_full, n_seq):
    T = xb.shape[0]
    S = T // n_seq
    seq = pl.BlockSpec((S, MIX), lambda b: (b, 0))
    return pl.pallas_call(
        _local_prompt_body,
        grid=(n_seq,),
        in_specs=[seq, seq, seq, _full((MIX, MIX)), _full((1, MIX)), _full(ws.shape), _full(bs_full.shape)],
        out_specs=[seq, seq],
        out_shape=[jax.ShapeDtypeStruct((T, MIX), F32)] * 2,
        scratch_shapes=[pltpu.VMEM((POOL_PAD + S, MIX), F32)],
        compiler_params=_params(("parallel",)),
        name="local_prompt",
    )(xb, uc, vn, wp_bd, sp, ws, bs_full)


def _log_sigmoids(z):
    ls = jnp.minimum(z, 0.0) - jnp.log(1.0 + jnp.exp(-jnp.abs(z)))
    return ls, ls - z


def _suffix_sums(lk, upper):
    hi = lk.astype(BF16)
    lo = (lk - hi.astype(F32)).astype(BF16)
    return (jnp.dot(hi, upper, preferred_element_type=F32)
            + jnp.dot(lo, upper, preferred_element_type=F32))


def _sb_prompt_body(q_ref, k_ref, v_ref, o_ref, acc_ref, car_ref, *, blk, scale):
    qi = pl.program_id(2)
    lane = lax.broadcasted_iota(jnp.int32, (1, LANES), 1)
    q = q_ref[...] * scale
    qh = [jnp.where(lane < HEAD, q, 0.0).astype(BF16), jnp.where(lane >= HEAD, q, 0.0).astype(BF16)]
    r = lax.broadcasted_iota(jnp.int32, (blk, blk), 0)
    c = lax.broadcasted_iota(jnp.int32, (blk, blk), 1)
    upper = (r > c).astype(BF16)
    causal = c < r

    def tile(j, diag):
        start = pl.multiple_of(j * blk, blk)
        k = k_ref[:, pl.ds(start, blk)].astype(BF16)
        v = v_ref[:, pl.ds(start, blk)].astype(BF16)
        for h in range(2):
            z = jnp.dot(qh[h], k, preferred_element_type=F32)
            ls, lk = _log_sigmoids(z)
            if diag:
                lk = jnp.where(causal, lk, 0.0)
                between = _suffix_sums(lk, upper)
            else:
                between = _suffix_sums(lk, upper) + car_ref[h]
            w = jnp.exp(ls + between)
            if diag:
                w = jnp.where(causal, w, 0.0)
            pv = lax.dot_general(w.astype(BF16), v, _NT, preferred_element_type=F32)
            tot = jnp.sum(lk, axis=-1, keepdims=True)
            if diag:
                acc_ref[h] = pv
                car_ref[h] = tot
            else:
                acc_ref[h] += pv
                car_ref[h] += tot

    tile(qi, True)

    def body(i, carry):
        tile(qi - 1 - i, False)
        return carry

    lax.fori_loop(0, qi, body, 0)
    o_ref[...] = jnp.where(lane < HEAD, acc_ref[0], acc_ref[1])


def _sb_prompt(q, k, v, n_seq):
    T = q.shape[0]
    S = T // n_seq
    blk = min(ATT_BLOCK, S)
    nq = S // blk
    pairs = MIX // LANES
    qspec = pl.BlockSpec((blk, LANES), lambda b, p, i: (b * nq + i, p))
    kspec = pl.BlockSpec((None, LANES, S), lambda b, p, i: (b, p, 0))
    return pl.pallas_call(
        functools.partial(_sb_prompt_body, blk=blk, scale=HEAD ** -0.5),
        grid=(n_seq, pairs, nq),
        in_specs=[qspec, kspec, kspec],
        out_specs=qspec,
        out_shape=jax.ShapeDtypeStruct((T, MIX), F32),
        scratch_shapes=[pltpu.VMEM((2, blk, LANES), F32), pltpu.VMEM((2, blk, 1), F32)],
        compiler_params=_params(("parallel", "parallel", "arbitrary")),
        name="sb_prompt",
    )(q, k, v)


def _diff_lambda(lam_ref, lam_init):
    s1 = jnp.sum(lam_ref[0:1, :] * lam_ref[1:2, :], axis=-1, keepdims=True)
    s2 = jnp.sum(lam_ref[2:3, :] * lam_ref[3:4, :], axis=-1, keepdims=True)
    return jnp.exp(s1) - jnp.exp(s2) + lam_init


def _head_rms(o, g, lane_in_head0, out_scale):
    o2 = o * o
    s0 = jnp.sum(jnp.where(lane_in_head0, o2, 0.0), axis=-1, keepdims=True)
    s1 = jnp.sum(jnp.where(lane_in_head0, 0.0, o2), axis=-1, keepdims=True)
    ms = jnp.where(lane_in_head0, s0, s1) * (1.0 / HEAD)
    return o * lax.rsqrt(ms + EPS) * g * out_scale


def _df_prompt_body(lam_ref, gs_ref, q_ref, k_ref, v_ref, o_ref, acc_ref, m_ref, l_ref, *, blk, scale, lam_init):
    qi = pl.program_id(2)
    lane = lax.broadcasted_iota(jnp.int32, (1, LANES), 1)
    q = q_ref[...] * scale
    qm = [jnp.where((lane >= i * DF_QK) & (lane < (i + 1) * DF_QK), q, 0.0).astype(BF16) for i in range(4)]
    r = lax.broadcasted_iota(jnp.int32, (blk, blk), 0)
    c = lax.broadcasted_iota(jnp.int32, (blk, blk), 1)
    causal = c <= r

    def tile(j, diag):
        start = pl.multiple_of(j * blk, blk)
        k = k_ref[:, pl.ds(start, blk)].astype(BF16)
        v = v_ref[:, pl.ds(start, blk)].astype(BF16)
        for i in range(4):
            s = jnp.dot(qm[i], k, preferred_element_type=F32)
            if diag:
                s = jnp.where(causal, s, NEG_INF)
                m_new = jnp.max(s, axis=-1, keepdims=True)
                p = jnp.exp(s - m_new)
                l_ref[i] = jnp.sum(p, axis=-1, keepdims=True)
                acc_ref[i] = lax.dot_general(p.astype(BF16), v, _NT, preferred_element_type=F32)
            else:
                m_old = m_ref[i]
                m_new = jnp.maximum(m_old, jnp.max(s, axis=-1, keepdims=True))
                a = jnp.exp(m_old - m_new)
                p = jnp.exp(s - m_new)
                l_ref[i] = a * l_ref[i] + jnp.sum(p, axis=-1, keepdims=True)
                acc_ref[i] = a * acc_ref[i] + lax.dot_general(p.astype(BF16), v, _NT,
                                                              preferred_element_type=F32)
            m_ref[i] = m_new

    tile(qi, True)

    def body(i, carry):
        tile(qi - 1 - i, False)
        return carry

    lax.fori_loop(0, qi, body, 0)
    lam = _diff_lambda(lam_ref, lam_init)
    o0 = acc_ref[0] / l_ref[0] - lam * (acc_ref[1] / l_ref[1])
    o1 = acc_ref[2] / l_ref[2] - lam * (acc_ref[3] / l_ref[3])
    o = jnp.where(lane < HEAD, o0, o1)
    o_ref[...] = _head_rms(o, gs_ref[...], lane < HEAD, 1.0 - lam_init)


def _df_prompt(lam_rows, gs2, q, k, v, n_seq, lam_init):
    T = q.shape[0]
    S = T // n_seq
    blk = min(ATT_BLOCK, S)
    nq = S // blk
    pairs = MIX // LANES
    qspec = pl.BlockSpec((blk, LANES), lambda b, p, i: (b * nq + i, p))
    kspec = pl.BlockSpec((None, LANES, S), lambda b, p, i: (b, p, 0))
    return pl.pallas_call(
        functools.partial(_df_prompt_body, blk=blk, scale=DF_QK ** -0.5, lam_init=lam_init),
        grid=(n_seq, pairs, nq),
        in_specs=[_full(lam_rows.shape), _full((1, LANES)), qspec, kspec, kspec],
        out_specs=qspec,
        out_shape=jax.ShapeDtypeStruct((T, MIX), F32),
        scratch_shapes=[pltpu.VMEM((4, blk, LANES), F32), pltpu.VMEM((4, blk, 1), F32),
                        pltpu.VMEM((4, blk, 1), F32)],
        compiler_params=_params(("parallel", "parallel", "arbitrary")),
        name="df_prompt",
    )(lam_rows, gs2, q, k, v)


def _merge_body(x_ref, g_ref, oa_ref, ob_ref, oc_ref, od_ref, wg_ref, wb_ref, wo_ref, out_ref):
    x = x_ref[...]
    D = x.shape[1]
    h = _rms(x, g_ref[...]).astype(BF16)
    acc = None
    for i, o_ref in enumerate((oa_ref, ob_ref, oc_ref, od_ref)):
        gate = jnp.dot(h, wg_ref[:, i * D:(i + 1) * D], preferred_element_type=F32)
        branch = jnp.dot(o_ref[...].astype(BF16), wb_ref[i], preferred_element_type=F32)
        t = branch / (1.0 + jnp.exp(-gate))
        acc = t if acc is None else acc + t
    out_ref[...] = x + jnp.dot(acc.astype(BF16), wo_ref[...], preferred_element_type=F32)


def _merge(x, g, oa, ob, oc, od, wg, wb, wo):
    T, D = x.shape
    tm = min(ROW_TILE, T)
    row = pl.BlockSpec((tm, D), lambda i: (i, 0))
    mix = pl.BlockSpec((tm, MIX), lambda i: (i, 0))
    return pl.pallas_call(
        _merge_body,
        grid=(T // tm,),
        in_specs=[row, _full((1, D)), mix, mix, mix, mix, _full(wg.shape), _full(wb.shape), _full(wo.shape)],
        out_specs=row,
        out_shape=jax.ShapeDtypeStruct((T, D), F32),
        compiler_params=_params(("parallel",)),
        name="merge",
    )(x, g, oa, ob, oc, od, wg, wb, wo)


def _memkv_body(m_ref, wt_ref, g_ref, bd_ref, k_ref, v_ref):
    yt = lax.dot_general(wt_ref[...], m_ref[...].astype(BF16), _NT, preferred_element_type=F32)
    k = yt[:MIX]
    ms = jnp.dot(bd_ref[...], (k * k).astype(BF16), preferred_element_type=F32)
    k_ref[...] = k * lax.rsqrt(ms + EPS) * g_ref[...]
    v_ref[...] = yt[MIX:]


def _memkv(mem, wt, g_col, bd64, n_seq):
    T, D = mem.shape
    M = T // n_seq
    out = pl.BlockSpec((None, MIX, M), lambda b: (b, 0, 0))
    return pl.pallas_call(
        _memkv_body,
        grid=(n_seq,),
        in_specs=[pl.BlockSpec((M, D), lambda b: (b, 0)), _full(wt.shape), _full((MIX, 1)), _full((MIX, MIX))],
        out_specs=[out, out],
        out_shape=[jax.ShapeDtypeStruct((n_seq, MIX, M), F32)] * 2,
        compiler_params=_params(("parallel",)),
        name="mem_kv",
    )(mem, wt, g_col, bd64)


def _cross_query(x, g, wq, gq, bd):
    h = _rms(x, g).astype(BF16)
    q = jnp.dot(h, wq, preferred_element_type=F32)
    return _group_rms(q, gq, bd) * HEAD ** -0.5


def _cross_heads(q, mk, mv):
    lane = lax.broadcasted_iota(jnp.int32, (1, LANES), 1)
    outs = []
    for p in range(MIX // LANES):
        qp = q[:, p * LANES:(p + 1) * LANES]
        kp = mk[p * LANES:(p + 1) * LANES, :].astype(BF16)
        vp = mv[p * LANES:(p + 1) * LANES, :].astype(BF16)
        oh = []
        for h in range(2):
            in_head = (lane >= h * HEAD) & (lane < (h + 1) * HEAD)
            s = jnp.dot(jnp.where(in_head, qp, 0.0).astype(BF16), kp, preferred_element_type=F32)
            e = jnp.exp(s - jnp.max(s, axis=-1, keepdims=True))
            o = lax.dot_general(e.astype(BF16), vp, _NT, preferred_element_type=F32)
            oh.append(o / jnp.sum(e, axis=-1, keepdims=True))
        outs.append(jnp.where(lane < HEAD, oh[0], oh[1]))
    return outs


def _cross_prompt_body(x_ref, g_ref, wq_ref, gq_ref, bd_ref, mk_ref, mv_ref, wo_ref, out_ref):
    x = x_ref[...]
    q = _cross_query(x, g_ref[...], wq_ref[...], gq_ref[...], bd_ref[...])
    o = jnp.concatenate(_cross_heads(q, mk_ref[...], mv_ref[...]), axis=-1)
    out_ref[...] = x + jnp.dot(o.astype(BF16), wo_ref[...], preferred_element_type=F32)


def _cross_prompt(x, g, wq, gq, bd64, mk, mv, wo, n_seq):
    T, D = x.shape
    S = T // n_seq
    M = mk.shape[2]
    tm = min(ROW_TILE, S)
    nt = S // tm
    row = pl.BlockSpec((tm, D), lambda b, i: (b * nt + i, 0))
    mem = pl.BlockSpec((None, MIX, M), lambda b, i: (b, 0, 0))
    return pl.pallas_call(
        _cross_prompt_body,
        grid=(n_seq, nt),
        in_specs=[row, _full((1, D)), _full(wq.shape), _full((1, MIX)), _full((MIX, MIX)), mem, mem,
                  _full(wo.shape)],
        out_specs=row,
        out_shape=jax.ShapeDtypeStruct((T, D), F32),
        compiler_params=_params(("parallel", "parallel")),
        name="cross_prompt",
    )(x, g, wq, gq, bd64, mk, mv, wo)


def _cross_sample_body(x_ref, g_ref, wq_ref, gq_ref, bd_ref, mk_ref, mv_ref, wo_ref, out_ref, q_scr, acc_scr,
                       *, nb, ts):
    j = pl.program_id(0)

    @pl.when(j == 0)
    def _():
        q_scr[...] = _cross_query(x_ref[...], g_ref[...], wq_ref[...], gq_ref[...], bd_ref[...])
        acc_scr[...] = jnp.zeros(acc_scr.shape, F32)

    rows = lax.broadcasted_iota(jnp.int32, (nb * ts, 1), 0)
    mine = rows == j
    for t in range(1, ts):
        mine = mine | (rows == t * nb + j)
    outs = _cross_heads(q_scr[...], mk_ref[...], mv_ref[...])
    for p, o in enumerate(outs):
        sl = slice(p * LANES, (p + 1) * LANES)
        acc_scr[:, sl] = jnp.where(mine, o, acc_scr[:, sl])

    @pl.when(j == nb - 1)
    def _():
        out_ref[...] = x_ref[...] + jnp.dot(acc_scr[...].astype(BF16), wo_ref[...], preferred_element_type=F32)


def _cross_sample(x, g, wq, gq, bd64, cache_k, cache_v, layer, wo, nb, ts):
    T, D = x.shape
    M = cache_k.shape[3]
    mem = pl.BlockSpec((None, None, MIX, M), lambda j: (layer, j, 0, 0))
    return pl.pallas_call(
        functools.partial(_cross_sample_body, nb=nb, ts=ts),
        grid=(nb,),
        in_specs=[_full((T, D)), _full((1, D)), _full(wq.shape), _full((1, MIX)), _full((MIX, MIX)), mem, mem,
                  _full(wo.shape)],
        out_specs=_full((T, D)),
        out_shape=jax.ShapeDtypeStruct((T, D), F32),
        scratch_shapes=[pltpu.VMEM((T, MIX), F32), pltpu.VMEM((T, MIX), F32)],
        compiler_params=_params(("arbitrary",)),
        name="cross_sample",
    )(x, g, wq, gq, bd64, cache_k, cache_v, wo)


def _conv_gate(a0, a1, a2, b, cw):
    ac = cw[3:4] + cw[0:1] * a2 + cw[1:2] * a1 + cw[2:3] * a0
    return ac / (1.0 + jnp.exp(-ac)) * b


def _ffn_prompt_body(x_ref, xh_ref, g_ref, wup_ref, cw_ref, wdn_ref, out_ref, cst_ref, h_scr, a_scr, acc_scr,
                     *, tm, F, tf, tiles_per_seq):
    halo = BF16_ROWS
    first = (pl.program_id(0) % tiles_per_seq) == 0
    x = x_ref[...]
    h_scr[0:halo, :] = _rms(xh_ref[...], g_ref[...]).astype(BF16)
    h_scr[halo:halo + tm, :] = _rms(x, g_ref[...]).astype(BF16)
    for j in range(F // tf):
        cols = slice(j * tf, (j + 1) * tf)
        a_scr[...] = jnp.dot(h_scr[...], wup_ref[:, cols], preferred_element_type=F32)
        b = jnp.dot(h_scr[halo:halo + tm, :], wup_ref[:, F + j * tf:F + (j + 1) * tf],
                    preferred_element_type=F32)

        @pl.when(first)
        def _():
            a_scr[0:halo, :] = jnp.zeros((halo, tf), F32)

        act = _conv_gate(a_scr[halo:halo + tm, :], a_scr[halo - 1:halo - 1 + tm, :],
                         a_scr[halo - 2:halo - 2 + tm, :], b, cw_ref[:, cols])
        contrib = jnp.dot(act.astype(BF16), wdn_ref[cols, :], preferred_element_type=F32)
        if j == 0:
            acc_scr[...] = contrib
        else:
            acc_scr[...] += contrib
        cst_ref[0, :, cols] = a_scr[tm:tm + halo, :]
    out_ref[...] = x + acc_scr[...]


def _ffn_prompt(x, g, wup, cw, wdn, n_seq):
    T, D = x.shape
    F = wdn.shape[0]
    S = T // n_seq
    tm = min(ROW_TILE, S)
    nt = S // tm
    halo = BF16_ROWS
    tf = FFN_CHUNK if F % FFN_CHUNK == 0 else LANES
    row = pl.BlockSpec((tm, D), lambda i: (i, 0))
    prev = pl.BlockSpec((halo, D), lambda i: (jnp.maximum(i * (tm // halo) - 1, 0), 0))
    return pl.pallas_call(
        functools.partial(_ffn_prompt_body, tm=tm, F=F, tf=tf, tiles_per_seq=nt),
        grid=(T // tm,),
        in_specs=[row, prev, _full((1, D)), _full(wup.shape), _full(cw.shape), _full(wdn.shape)],
        out_specs=[row, pl.BlockSpec((1, halo, F), lambda i: (i // nt, 0, 0))],
        out_shape=[jax.ShapeDtypeStruct((T, D), F32), jax.ShapeDtypeStruct((n_seq, halo, F), F32)],
        scratch_shapes=[pltpu.VMEM((halo + tm, D), BF16), pltpu.VMEM((halo + tm, tf), F32),
                        pltpu.VMEM((tm, D), F32)],
        compiler_params=_params(("arbitrary",)),
        name="ffn_prompt",
    )(x, x, g, wup, cw, wdn)


def _ffn_sample_body(x_ref, prev_ref, g_ref, wup_ref, cw_ref, wdn_ref, out_ref, cst_ref, *, nb, F, tf):
    x = x_ref[...]
    T = x.shape[0]
    h = _rms(x, g_ref[...]).astype(BF16)
    acc = None
    for j in range(F // tf):
        cols = slice(j * tf, (j + 1) * tf)
        a = jnp.dot(h, wup_ref[:, cols], preferred_element_type=F32)
        b = jnp.dot(h, wup_ref[:, F + j * tf:F + (j + 1) * tf], preferred_element_type=F32)
        prev = prev_ref[:, cols]
        a1 = jnp.concatenate([prev[nb:2 * nb], a[:T - nb]], axis=0)
        a2 = jnp.concatenate([prev, a[:T - 2 * nb]], axis=0)
        act = _conv_gate(a, a1, a2, b, cw_ref[:, cols])
        contrib = jnp.dot(act.astype(BF16), wdn_ref[cols, :], preferred_element_type=F32)
        acc = contrib if acc is None else acc + contrib
        cst_ref[:, cols] = a[T - 2 * nb:]
    out_ref[...] = x + acc


def _ffn_sample(x, prev, g, wup, cw, wdn, nb):
    T, D = x.shape
    F = wdn.shape[0]
    tf = FFN_CHUNK if F % FFN_CHUNK == 0 else LANES
    return pl.pallas_call(
        functools.partial(_ffn_sample_body, nb=nb, F=F, tf=tf),
        grid=(1,),
        in_specs=[_full((T, D)), _full(prev.shape), _full((1, D)), _full(wup.shape), _full(cw.shape),
                  _full(wdn.shape)],
        out_specs=[_full((T, D)), _full(prev.shape)],
        out_shape=[jax.ShapeDtypeStruct((T, D), F32), jax.ShapeDtypeStruct(prev.shape, F32)],
        compiler_params=_params(("arbitrary",)),
        name="ffn_sample",
    )(x, prev, g, wup, cw, wdn)


def _local_sample_body(xb_ref, st_ref, uc_ref, vn_ref, wp_ref, sp_ref, coef_ref, bias_ref, ob_ref, oc_ref,
                       *, nb, ts, n_past):
    wl = _lane_window()
    ext = jnp.concatenate([st_ref[...], xb_ref[...]], axis=0)
    base = POOL_STATE * nb
    x = xb_ref[...]
    acc = x
    for k in range(1, POOL_STATE + 1):
        acc = acc + jnp.where(k < wl, ext[base - k * nb:base - k * nb + ts * nb], 0.0)
    for t in range(ts):
        rows = slice(t * nb, (t + 1) * nb)
        cnt = jnp.minimum(n_past + t + 1, wl).astype(F32)
        d = acc[rows] / cnt - x[rows]
        y = jnp.dot(d.astype(BF16), wp_ref[...], preferred_element_type=F32)
        ob_ref[rows, :] = y * sp_ref[...]
        mixed = bias_ref[t:t + 1, :]
        for s in range(t + 1):
            mixed = mixed + coef_ref[t * ts + s:t * ts + s + 1, :] * vn_ref[s * nb:(s + 1) * nb, :]
        oc_ref[rows, :] = uc_ref[rows, :] * mixed


def _local_sample(xb, state, uc, vn, wp_bd, sp, coef, bias, nb, ts, n_past):
    T = xb.shape[0]
    return pl.pallas_call(
        functools.partial(_local_sample_body, nb=nb, ts=ts, n_past=n_past),
        grid=(1,),
        in_specs=[_full((T, MIX)), _full(state.shape), _full((T, MIX)), _full((T, MIX)), _full((MIX, MIX)),
                  _full((1, MIX)), _full(coef.shape), _full(bias.shape)],
        out_specs=[_full((T, MIX))] * 2,
        out_shape=[jax.ShapeDtypeStruct((T, MIX), F32)] * 2,
        compiler_params=_params(("arbitrary",)),
        name="local_sample",
    )(xb, state, uc, vn, wp_bd, sp, coef, bias)


Q_PAD = 8


def _slot_queries(q8, slot_width):
    lane = lax.broadcasted_iota(jnp.int32, (1, MIX), 1)
    slots = [jnp.where((lane >= i * slot_width) & (lane < (i + 1) * slot_width), q8, 0.0)
             for i in range(MIX // slot_width)]
    return jnp.concatenate(slots, axis=0).astype(BF16)


def _stage_pages(page_refs, stage_ref):
    n = page_refs[0].shape[1]
    for g, ref in enumerate(page_refs):
        stage_ref[:, g * n:(g + 1) * n] = ref[...].astype(BF16)


def _sb_decode_body(pt_ref, q_ref, kn_ref, vn_ref, *rest, G, scale):
    k_refs, v_refs = rest[:G], rest[G:2 * G]
    o_ref, kst, vst, acc_ref, car_ref = rest[2 * G:]
    j = pl.program_id(1)
    rows = (MIX // HEAD) * Q_PAD
    qbd = _slot_queries(q_ref[0] * scale, HEAD)
    seg = ATT_BLOCK
    r = lax.broadcasted_iota(jnp.int32, (seg, seg), 0)
    c = lax.broadcasted_iota(jnp.int32, (seg, seg), 1)
    upper = (r > c).astype(BF16)

    def segment(kb, vb, up, valid):
        z = jnp.dot(qbd, kb, preferred_element_type=F32)
        ls, lk = _log_sigmoids(z)
        if valid is not None:
            lk = jnp.where(valid, lk, 0.0)
        w = jnp.exp(ls + _suffix_sums(lk, up) + car_ref[...])
        if valid is not None:
            w = jnp.where(valid, w, 0.0)
        acc_ref[...] += lax.dot_general(w.astype(BF16), vb, _NT, preferred_element_type=F32)
        car_ref[...] += jnp.sum(lk, axis=-1, keepdims=True)

    @pl.when(j == 0)
    def _():
        acc_ref[...] = jnp.zeros(acc_ref.shape, F32)
        car_ref[...] = jnp.zeros(car_ref.shape, F32)
        n = kn_ref.shape[2]
        t = lax.broadcasted_iota(jnp.int32, (rows, n), 0) & (Q_PAD - 1)
        s = lax.broadcasted_iota(jnp.int32, (rows, n), 1)
        segment(kn_ref[0].astype(BF16), vn_ref[0].astype(BF16), upper[:n, :n], s < t)

    _stage_pages(k_refs, kst)
    _stage_pages(v_refs, vst)
    nk = kst.shape[1]
    for sidx in range(nk // seg - 1, -1, -1):
        segment(kst[:, sidx * seg:(sidx + 1) * seg], vst[:, sidx * seg:(sidx + 1) * seg], upper, None)

    @pl.when(j == pl.num_programs(1) - 1)
    def _():
        lane = lax.broadcasted_iota(jnp.int32, (1, MIX), 1)
        o = jnp.zeros((Q_PAD, MIX), F32)
        for h in range(MIX // HEAD):
            in_head = (lane >= h * HEAD) & (lane < (h + 1) * HEAD)
            o = jnp.where(in_head, acc_ref[h * Q_PAD:(h + 1) * Q_PAD, :], o)
        o_ref[0] = o


def _df_decode_body(pt_ref, lam_ref, gs_ref, q_ref, kn_ref, vn_ref, *rest, G, scale, lam_init):
    k_refs, v_refs = rest[:G], rest[G:2 * G]
    o_ref, kst, vst, acc_ref, m_ref, l_ref = rest[2 * G:]
    j = pl.program_id(1)
    slots = MIX // DF_QK
    rows = slots * Q_PAD
    qbd = _slot_queries(q_ref[0] * scale, DF_QK)

    def block(kb, vb, valid):
        s = jnp.dot(qbd, kb, preferred_element_type=F32)
        if valid is not None:
            s = jnp.where(valid, s, NEG_INF)
        m_old = m_ref[...]
        m_new = jnp.maximum(m_old, jnp.max(s, axis=-1, keepdims=True))
        a = jnp.exp(m_old - m_new)
        p = jnp.exp(s - m_new)
        l_ref[...] = a * l_ref[...] + jnp.sum(p, axis=-1, keepdims=True)
        acc_ref[...] = a * acc_ref[...] + lax.dot_general(p.astype(BF16), vb, _NT, preferred_element_type=F32)
        m_ref[...] = m_new

    @pl.when(j == 0)
    def _():
        acc_ref[...] = jnp.zeros(acc_ref.shape, F32)
        l_ref[...] = jnp.zeros(l_ref.shape, F32)
        m_ref[...] = jnp.full(m_ref.shape, NEG_INF, F32)
        n = kn_ref.shape[2]
        t = lax.broadcasted_iota(jnp.int32, (rows, n), 0) & (Q_PAD - 1)
        s = lax.broadcasted_iota(jnp.int32, (rows, n), 1)
        block(kn_ref[0].astype(BF16), vn_ref[0].astype(BF16), s <= t)

    _stage_pages(k_refs, kst)
    _stage_pages(v_refs, vst)
    block(kst[...], vst[...], None)

    @pl.when(j == pl.num_programs(1) - 1)
    def _():
        lane = lax.broadcasted_iota(jnp.int32, (1, MIX), 1)
        lam = _diff_lambda(lam_ref, lam_init)
        norm = acc_ref[...] / l_ref[...]
        o = jnp.zeros((Q_PAD, MIX), F32)
        for h in range(MIX // HEAD):
            in_head = (lane >= h * HEAD) & (lane < (h + 1) * HEAD)
            r0 = (2 * h) * Q_PAD
            oh = norm[r0:r0 + Q_PAD, :] - lam * norm[r0 + Q_PAD:r0 + 2 * Q_PAD, :]
            o = jnp.where(in_head, oh, o)
        lane2 = lax.broadcasted_iota(jnp.int32, (1, LANES), 1)
        parts = [_head_rms(o[:, p * LANES:(p + 1) * LANES], gs_ref[...], lane2 < HEAD, 1.0 - lam_init)
                 for p in range(MIX // LANES)]
        o_ref[0] = jnp.concatenate(parts, axis=-1)


def _decode_specs(cache_k, layer, nb, n_pages, G):
    nj = n_pages // G
    page = cache_k.shape[3]

    def page_spec(g, reverse):
        def index(b, j, pt):
            jj = nj - 1 - j if reverse else j
            return (layer, pt[b, jj * G + g], 0, 0)
        return pl.BlockSpec((None, None, MIX, page), index)

    tok = pl.BlockSpec((1, Q_PAD, MIX), lambda b, j, pt: (b, 0, 0))
    new = pl.BlockSpec((1, MIX, LANES), lambda b, j, pt: (b, 0, 0))
    return nj, page, page_spec, tok, new


def _sb_decode(page_table, q, k_new, v_new, cache_k, cache_v, layer):
    nb, n_pages = page_table.shape
    G = PAGES_PER_STEP
    nj, page, page_spec, tok, new = _decode_specs(cache_k, layer, nb, n_pages, G)
    rows = (MIX // HEAD) * Q_PAD
    grid_spec = pltpu.PrefetchScalarGridSpec(
        num_scalar_prefetch=1,
        grid=(nb, nj),
        in_specs=[tok, new, new] + [page_spec(g, True) for g in range(G)] * 2,
        out_specs=tok,
        scratch_shapes=[pltpu.VMEM((MIX, G * page), BF16), pltpu.VMEM((MIX, G * page), BF16),
                        pltpu.VMEM((rows, MIX), F32), pltpu.VMEM((rows, 1), F32)],
    )
    return pl.pallas_call(
        functools.partial(_sb_decode_body, G=G, scale=HEAD ** -0.5),
        grid_spec=grid_spec,
        out_shape=jax.ShapeDtypeStruct((nb, Q_PAD, MIX), F32),
        compiler_params=_params(("parallel", "arbitrary")),
        name="sb_decode",
    )(page_table, q, k_new, v_new, *([cache_k] * G), *([cache_v] * G))


def _df_decode(page_table, lam_rows, gs2, q, k_new, v_new, cache_k, cache_v, layer, lam_init):
    nb, n_pages = page_table.shape
    G = PAGES_PER_STEP
    nj, page, page_spec, tok, new = _decode_specs(cache_k, layer, nb, n_pages, G)
    rows = (MIX // DF_QK) * Q_PAD
    const = lambda shape: pl.BlockSpec(shape, lambda b, j, pt: (0,) * len(shape))
    grid_spec = pltpu.PrefetchScalarGridSpec(
        num_scalar_prefetch=1,
        grid=(nb, nj),
        in_specs=[const(lam_rows.shape), const((1, LANES)), tok, new, new]
        + [page_spec(g, False) for g in range(G)] * 2,
        out_specs=tok,
        scratch_shapes=[pltpu.VMEM((MIX, G * page), BF16), pltpu.VMEM((MIX, G * page), BF16),
                        pltpu.VMEM((rows, MIX), F32), pltpu.VMEM((rows, 1), F32), pltpu.VMEM((rows, 1), F32)],
    )
    return pl.pallas_call(
        functools.partial(_df_decode_body, G=G, scale=DF_QK ** -0.5, lam_init=lam_init),
        grid_spec=grid_spec,
        out_shape=jax.ShapeDtypeStruct((nb, Q_PAD, MIX), F32),
        compiler_params=_params(("parallel", "arbitrary")),
        name="df_decode",
    )(page_table, lam_rows, gs2, q, k_new, v_new, *([cache_k] * G), *([cache_v] * G))


def _row(v):
    return v.reshape(1, -1).astype(F32)


def _tile_row(v, width):
    return jnp.tile(v.astype(F32), width // v.shape[0]).reshape(1, width)


def _pad_rows(a, rows):
    return jnp.pad(a, ((0, rows - a.shape[0]),) + ((0, 0),) * (a.ndim - 1))


def _block_diag(w):
    G, c, d = w.shape
    eye = jnp.eye(G, dtype=w.dtype)
    return (eye[:, None, :, None] * w[:, :, None, :]).reshape(G * c, G * d)


def kernel(x_prompt, x_sample, mem_prompt, cache_sb_k, cache_sb_v, cache_df_k, cache_df_v, cache_mem_k, cache_mem_v, state_pool, state_ffn_conv, page_table, g_mix, w_in, w_branch, w_out, w_pool, s_pool, g_gm, w_spatial, b_spatial, g_qd, g_kd, lam_q1, lam_k1, lam_q2, lam_k2, g_subln, g_cross, w_cq, w_ckv, w_co, g_cq, g_ck, g_ffn, w_up, w_conv, b_conv, w_down):
    B, S, D = x_prompt.shape
    NB, TS, _ = x_sample.shape
    depth = w_in.shape[0]
    M = mem_prompt.shape[1]
    F = w_down.shape[1]
    page = cache_sb_k.shape[2]
    n_pages = page_table.shape[1]
    n_past = n_pages * page
    heads = MIX // HEAD
    assert TS >= CONV_W - 1 and TS <= Q_PAD and TS <= GM_CHUNK and n_pages % PAGES_PER_STEP == 0

    bd32 = _block_diag_mean(MIX, DF_QK)
    bd64 = _block_diag_mean(MIX, HEAD)
    feat_major = lambda c: c.transpose(0, 1, 3, 4, 2).reshape(c.shape[0], c.shape[1], MIX, c.shape[2])
    csb_k, csb_v, cdf_k, cdf_v = (feat_major(c) for c in (cache_sb_k, cache_sb_v, cache_df_k, cache_df_v))
    cmem_k, cmem_v = feat_major(cache_mem_k), feat_major(cache_mem_v)
    tok_major = lambda a: a.reshape(a.shape[0], heads, HEAD, a.shape[2]).transpose(0, 3, 1, 2)

    xp = x_prompt.reshape(B * S, D)
    xs = x_sample.transpose(1, 0, 2).reshape(TS * NB, D)
    mem = mem_prompt.reshape(B * M, D)

    to_batch = lambda a: a.reshape(TS, NB, MIX).transpose(1, 0, 2)
    to_time = lambda a: a[:, :TS].transpose(1, 0, 2).reshape(TS * NB, MIX)
    pad_tok = lambda a, n: jnp.pad(to_batch(a), ((0, 0), (0, n - TS), (0, 0)))
    new_page = lambda a: jnp.pad(to_batch(a).transpose(0, 2, 1), ((0, 0), (0, 0), (0, LANES - TS)))
    kv_cols = lambda w: jnp.concatenate([w[:, i * MIX:(i + 1) * MIX] for i in (1, 2, 7, 8)], axis=1)
    tok_cols = lambda w: jnp.concatenate([w[:, i * MIX:(i + 1) * MIX] for i in (0, 3, 4, 5, 6)], axis=1)

    outs = [[] for _ in range(15)]
    for l in range(depth):
        lam_init = 0.8 - 0.6 * math.exp(-0.3 * l)
        w_l = w_in[l]
        wtok = tok_cols(w_l).astype(BF16)
        wkv = kv_cols(w_l).astype(BF16)
        wkv_t = wkv.T
        wg = w_l[:, 9 * MIX:].astype(BF16)
        wb = w_branch[l].astype(BF16)
        wo = w_out[l].astype(BF16)
        gq, gk = _tile_row(g_qd[l], MIX), _tile_row(g_kd[l], MIX)
        gk_col = gk.reshape(MIX, 1)
        ggm, gmix = _row(g_gm[l]), _row(g_mix[l])
        wp_bd = _block_diag(w_pool[l]).astype(BF16)
        sp = _row(s_pool[l])
        bs_full = jnp.repeat(b_spatial[l].T, MIX // GM_GROUPS, axis=1).astype(F32)
        lam_rows = _pad_rows(jnp.stack([lam_q1[l], lam_k1[l], lam_q2[l], lam_k2[l]]).astype(F32), 8)
        gs2 = _tile_row(g_subln[l], LANES)
        wcq = w_cq[l].astype(BF16)
        wckv_t = w_ckv[l].astype(BF16).T
        wco = w_co[l].astype(BF16)
        gcq, gck_col = _tile_row(g_cq[l], MIX), _tile_row(g_ck[l], MIX).reshape(MIX, 1)
        wup = w_up[l].astype(BF16)
        wdn = w_down[l].astype(BF16)
        cw = _pad_rows(jnp.concatenate([w_conv[l], b_conv[l][None]], axis=0).astype(F32), 8)

        qa, xb, uc, vn, qd, ka, va, kd, vd = _proj_in(xp, gmix, wtok, wkv_t, gq, gk_col, ggm, bd32, B, True)
        oa = _sb_prompt(qa, ka, va, B)
        ob, oc = _local_prompt(xb, uc, vn, wp_bd, sp, w_spatial[l].astype(F32), bs_full, B)
        od = _df_prompt(lam_rows, gs2, qd, kd, vd, B, lam_init)
        xp = _merge(xp, gmix, oa, ob, oc, od, wg, wb, wo)
        mk, mv = _memkv(mem, wckv_t, gck_col, bd64, B)
        xp = _cross_prompt(xp, _row(g_cross[l]), wcq, gcq, bd64, mk, mv, wco, B)
        xp, cst = _ffn_prompt(xp, _row(g_ffn[l]), wup, cw, wdn, B)
        outs[0].append(tok_major(ka))
        outs[1].append(tok_major(va))
        outs[2].append(tok_major(kd))
        outs[3].append(tok_major(vd))
        outs[4].append(tok_major(mk))
        outs[5].append(tok_major(mv))
        outs[6].append(xb.reshape(B, S, MIX)[:, S - POOL_STATE:])
        outs[7].append(cst[:, -(CONV_W - 1):])

        qa, xb, uc, vn, qd, ka, va, kd, vd = _proj_in(xs, gmix, wtok, wkv, gq, gk, ggm, bd32, 1, False)
        oa = _sb_decode(page_table, pad_tok(qa, Q_PAD), new_page(ka), new_page(va), csb_k, csb_v, l)
        od = _df_decode(page_table, lam_rows, gs2, pad_tok(qd, Q_PAD), new_page(kd), new_page(vd),
                        cdf_k, cdf_v, l, lam_init)
        state_t = state_pool[l].astype(F32).transpose(1, 0, 2).reshape(POOL_STATE * NB, MIX)
        ws_t = jnp.where(jnp.tril(jnp.ones((TS, TS), bool)), w_spatial[l][:, :TS, :TS], 0.0)
        coef = _pad_rows(jnp.repeat(ws_t.transpose(1, 2, 0).reshape(TS * TS, GM_GROUPS),
                                    MIX // GM_GROUPS, axis=1).astype(F32), -(-TS * TS // 8) * 8)
        bias = _pad_rows(bs_full[:TS], 8)
        ob, oc = _local_sample(xb, state_t, uc, vn, wp_bd, sp, coef, bias, NB, TS, n_past)
        xs = _merge(xs, gmix, to_time(oa), ob, oc, to_time(od), wg, wb, wo)
        xs = _cross_sample(xs, _row(g_cross[l]), wcq, gcq, bd64, cmem_k, cmem_v, l, wco, NB, TS)
        prev_t = state_ffn_conv[l].astype(F32).transpose(1, 0, 2).reshape((CONV_W - 1) * NB, F)
        xs, cst_s = _ffn_sample(xs, prev_t, _row(g_ffn[l]), wup, cw, wdn, NB)
        heads_s = lambda a: to_batch(a).reshape(NB, TS, heads, HEAD)
        outs[8].append(heads_s(ka))
        outs[9].append(heads_s(va))
        outs[10].append(heads_s(kd))
        outs[11].append(heads_s(vd))
        pool_ext = jnp.concatenate([state_pool[l].astype(F32), to_batch(xb)], axis=1)
        outs[12].append(pool_ext[:, -POOL_STATE:])
        outs[13].append(cst_s.reshape(CONV_W - 1, NB, F).transpose(1, 0, 2))
        outs[14].append(to_batch(vn))

    y_prompt = xp.reshape(B, S, D)
    y_sample = xs.reshape(TS, NB, D).transpose(1, 0, 2)
    return (y_prompt, y_sample) + tuple(jnp.stack(o) for o in outs)
```

```python
import functools
import math

import numpy as np
import jax
import jax.numpy as jnp
from jax import lax
from jax.experimental import pallas as pl
from jax.experimental.pallas import tpu as pltpu

F32 = jnp.float32
BF16 = jnp.bfloat16

EPS = 1e-6
NEG_INF = -1e30
LOG2_E = math.log2(math.e)
MIX = 256
HEAD = 64
DF_QK = 32
POOL_WINDOWS = (2, 4, 8, 16)
POOL_STATE = max(POOL_WINDOWS) - 1
POOL_PAD = 16
GM_CHUNK = 128
GM_GROUPS = 4
CONV_W = 3
LANES = 128
BF16_ROWS = 16
ROW_TILE = 512
ATT_BLOCK = 256
FFN_CHUNK = 256
PAGES_PER_STEP = 16
DF_RAW_EXP_BOUND = 40.0
VMEM_LIMIT = 56 * 1024 * 1024

_NT = (((1,), (1,)), ((), ()))


def _params(sem):
    return pltpu.CompilerParams(dimension_semantics=sem, vmem_limit_bytes=VMEM_LIMIT)


def _rms(x, g):
    return x * lax.rsqrt(jnp.mean(x * x, axis=-1, keepdims=True) + EPS) * g


def _group_rms(v, g, bd):
    ms = jnp.dot((v * v).astype(BF16), bd, preferred_element_type=F32)
    return v * lax.rsqrt(ms + EPS) * g


def _block_diag_mean(width, group):
    i = np.arange(width)
    return jnp.asarray((i[:, None] // group == i[None, :] // group) / group, BF16)


def _full(shape):
    return pl.BlockSpec(shape, lambda *_: (0,) * len(shape))


def _lane_window():
    lane = lax.broadcasted_iota(jnp.int32, (1, MIX), 1)
    gdim = MIX // len(POOL_WINDOWS)
    wl = jnp.full((1, MIX), POOL_WINDOWS[-1], jnp.int32)
    for g in range(len(POOL_WINDOWS) - 2, -1, -1):
        wl = jnp.where(lane < (g + 1) * gdim, POOL_WINDOWS[g], wl)
    return wl


def _proj_in_body(x_ref, g_ref, wtok_ref, wkv_ref, gq_ref, gk_ref, ggm_ref, bd_ref,
                  qa_ref, xb_ref, uc_ref, vn_ref, qd_ref, ka_ref, va_ref, kd_ref, vd_ref, *, kv_major):
    h = _rms(x_ref[...], g_ref[...]).astype(BF16)
    y = jnp.dot(h, wtok_ref[...], preferred_element_type=F32)
    part = lambda i: y[:, i * MIX:(i + 1) * MIX]
    qa_ref[...] = part(0)
    xb_ref[...] = part(1)
    uc_ref[...] = part(2)
    vn_ref[...] = _rms(part(3), ggm_ref[...])
    qd_ref[...] = _group_rms(part(4), gq_ref[...], bd_ref[...])
    if kv_major:
        yt = lax.dot_general(wkv_ref[...], h, _NT, preferred_element_type=F32)
        kd = yt[2 * MIX:3 * MIX]
        ms = jnp.dot(bd_ref[...], (kd * kd).astype(BF16), preferred_element_type=F32)
        ka_ref[...] = yt[0:MIX]
        va_ref[...] = yt[MIX:2 * MIX]
        kd_ref[...] = kd * lax.rsqrt(ms + EPS) * gk_ref[...]
        vd_ref[...] = yt[3 * MIX:4 * MIX]
    else:
        y2 = jnp.dot(h, wkv_ref[...], preferred_element_type=F32)
        ka_ref[...] = y2[:, 0:MIX]
        va_ref[...] = y2[:, MIX:2 * MIX]
        kd_ref[...] = _group_rms(y2[:, 2 * MIX:3 * MIX], gk_ref[...], bd_ref[...])
        vd_ref[...] = y2[:, 3 * MIX:4 * MIX]


def _proj_in(x, g, wtok, wkv, gq, gk, ggm, bd32, n_seq, kv_major):
    T, D = x.shape
    S = T // n_seq
    tm = min(ROW_TILE, S)
    nt = S // tm
    row = pl.BlockSpec((tm, D), lambda i: (i, 0))
    out = pl.BlockSpec((tm, MIX), lambda i: (i, 0))
    tok_shape = jax.ShapeDtypeStruct((T, MIX), F32)
    if kv_major:
        kv_spec = pl.BlockSpec((None, MIX, tm), lambda i: (i // nt, 0, i % nt))
        kv_shape = jax.ShapeDtypeStruct((n_seq, MIX, S), F32)
    else:
        kv_spec, kv_shape = out, tok_shape
    return pl.pallas_call(
        functools.partial(_proj_in_body, kv_major=kv_major),
        grid=(T // tm,),
        in_specs=[row, _full((1, D)), _full(wtok.shape), _full(wkv.shape), _full((1, MIX)), _full(gk.shape),
                  _full((1, MIX)), _full((MIX, MIX))],
        out_specs=[out] * 5 + [kv_spec] * 4,
        out_shape=[tok_shape] * 5 + [kv_shape] * 4,
        compiler_params=_params(("parallel",)),
        name="proj_in",
    )(x, g, wtok, wkv, gq, gk, ggm, bd32)


def _local_prompt_body(xb_ref, uc_ref, vn_ref, wp_ref, sp_ref, ws_ref, bs_ref, ob_ref, oc_ref, ext_ref):
    S = xb_ref.shape[0]
    ext_ref[0:POOL_PAD, :] = jnp.zeros((POOL_PAD, MIX), F32)
    ext_ref[POOL_PAD:POOL_PAD + S, :] = xb_ref[...]
    wl = _lane_window()
    lane = lax.broadcasted_iota(jnp.int32, (1, MIX), 1)
    R = min(256, S)
    for c in range(S // R):
        x = xb_ref[c * R:(c + 1) * R, :]
        acc = x
        for k in range(1, POOL_STATE + 1):
            lo = POOL_PAD + c * R - k
            acc = acc + jnp.where(k < wl, ext_ref[lo:lo + R, :], 0.0)
        pos = c * R + lax.broadcasted_iota(jnp.int32, (R, 1), 0)
        cnt = jnp.minimum(pos + 1, wl).astype(F32)
        d = acc / cnt - x
        y = jnp.dot(d.astype(BF16), wp_ref[...], preferred_element_type=F32)
        ob_ref[c * R:(c + 1) * R, :] = y * sp_ref[...]

    C = min(GM_CHUNK, S)
    r = lax.broadcasted_iota(jnp.int32, (C, C), 0)
    cidx = lax.broadcasted_iota(jnp.int32, (C, C), 1)
    gdim = MIX // GM_GROUPS
    ws = [jnp.where(r >= cidx, ws_ref[g][:C, :C], 0.0).astype(BF16) for g in range(GM_GROUPS)]
    bias = bs_ref[0:C, :]
    for n in range(S // C):
        vn = vn_ref[n * C:(n + 1) * C, :].astype(BF16)
        mixed = jnp.dot(ws[0], vn, preferred_element_type=F32)
        for g in range(1, GM_GROUPS):
            mixed = jnp.where(lane >= g * gdim, jnp.dot(ws[g], vn, preferred_element_type=F32), mixed)
        oc_ref[n * C:(n + 1) * C, :] = uc_ref[n * C:(n + 1) * C, :] * (mixed + bias)


def _local_prompt(xb, uc, vn, wp_bd, sp, ws, bs_full, n_seq):
    T = xb.shape[0]
    S = T // n_seq
    seq = pl.BlockSpec((S, MIX), lambda b: (b, 0))
    return pl.pallas_call(
        _local_prompt_body,
        grid=(n_seq,),
        in_specs=[seq, seq, seq, _full((MIX, MIX)), _full((1, MIX)), _full(ws.shape), _full(bs_full.shape)],
        out_specs=[seq, seq],
        out_shape=[jax.ShapeDtypeStruct((T, MIX), F32)] * 2,
        scratch_shapes=[pltpu.VMEM((POOL_PAD + S, MIX), F32)],
        compiler_params=_params(("parallel",)),
        name="local_prompt",
    )(xb, uc, vn, wp_bd, sp, ws, bs_full)


def _log_sigmoids(z):
    ls = jnp.minimum(z, 0.0) - jnp.log(1.0 + jnp.exp(-jnp.abs(z)))
    return ls, ls - z


def _suffix_sums(lk, upper):
    hi = lk.astype(BF16)
    lo = (lk - hi.astype(F32)).astype(BF16)
    return (jnp.dot(hi, upper, preferred_element_type=F32)
            + jnp.dot(lo, upper, preferred_element_type=F32))


def _log2_sigmoids(z2):
    ls2 = jnp.minimum(z2, 0.0) - jnp.log2(1.0 + jnp.exp2(-jnp.abs(z2)))
    return ls2, ls2 - z2


def _sb_prompt_body(q_ref, k_ref, v_ref, o_ref, acc_ref, car_ref, *, blk, scale):
    qi = pl.program_id(2)
    lane = lax.broadcasted_iota(jnp.int32, (1, LANES), 1)
    q = q_ref[...] * (scale * LOG2_E)
    qh = [jnp.where(lane < HEAD, q, 0.0).astype(BF16), jnp.where(lane >= HEAD, q, 0.0).astype(BF16)]
    r = lax.broadcasted_iota(jnp.int32, (blk, blk), 0)
    c = lax.broadcasted_iota(jnp.int32, (blk, blk), 1)
    upper = (r > c).astype(BF16)
    causal = c < r

    def tile(j, diag):
        start = pl.multiple_of(j * blk, blk)
        k = k_ref[:, pl.ds(start, blk)].astype(BF16)
        v = v_ref[:, pl.ds(start, blk)].astype(BF16)
        heads = range(2)
        zs = [jnp.dot(qh[h], k, preferred_element_type=F32) for h in heads]
        lss, lks = zip(*[_log2_sigmoids(z) for z in zs])
        if diag:
            lks = [jnp.where(causal, lk, 0.0) for lk in lks]
        bts = [jnp.dot(lks[h].astype(BF16), upper, preferred_element_type=F32) for h in heads]
        ws = [jnp.exp2(lss[h] + bts[h]) for h in heads]
        if diag:
            ws = [jnp.where(causal, w, 0.0) for w in ws]
        pvs = [lax.dot_general(ws[h].astype(BF16), v, _NT, preferred_element_type=F32) for h in heads]
        for h in heads:
            tot = bts[h][:, 0:1] + lks[h][:, 0:1]
            if diag:
                acc_ref[h] = pvs[h]
                car_ref[h] = tot
            else:
                acc_ref[h] += jnp.exp2(car_ref[h]) * pvs[h]
                car_ref[h] += tot

    tile(qi, True)

    def body(i, carry):
        tile(qi - 1 - i, False)
        return carry

    lax.fori_loop(0, qi, body, 0)
    o_ref[...] = jnp.where(lane < HEAD, acc_ref[0], acc_ref[1])


def _sb_prompt(q, k, v, n_seq):
    T = q.shape[0]
    S = T // n_seq
    blk = min(ATT_BLOCK, S)
    nq = S // blk
    pairs = MIX // LANES
    qspec = pl.BlockSpec((blk, LANES), lambda b, p, i: (b * nq + i, p))
    kspec = pl.BlockSpec((None, LANES, S), lambda b, p, i: (b, p, 0))
    return pl.pallas_call(
        functools.partial(_sb_prompt_body, blk=blk, scale=HEAD ** -0.5),
        grid=(n_seq, pairs, nq),
        in_specs=[qspec, kspec, kspec],
        out_specs=qspec,
        out_shape=jax.ShapeDtypeStruct((T, MIX), F32),
        scratch_shapes=[pltpu.VMEM((2, blk, LANES), F32), pltpu.VMEM((2, blk, 1), F32)],
        compiler_params=_params(("parallel", "parallel", "arbitrary")),
        name="sb_prompt",
    )(q, k, v)


def _diff_lambda(lam_ref, lam_init):
    s1 = jnp.sum(lam_ref[0:1, :] * lam_ref[1:2, :], axis=-1, keepdims=True)
    s2 = jnp.sum(lam_ref[2:3, :] * lam_ref[3:4, :], axis=-1, keepdims=True)
    return jnp.exp(s1) - jnp.exp(s2) + lam_init


def _head_rms(o, g, lane_in_head0, out_scale):
    o2 = o * o
    s0 = jnp.sum(jnp.where(lane_in_head0, o2, 0.0), axis=-1, keepdims=True)
    s1 = jnp.sum(jnp.where(lane_in_head0, 0.0, o2), axis=-1, keepdims=True)
    ms = jnp.where(lane_in_head0, s0, s1) * (1.0 / HEAD)
    return o * lax.rsqrt(ms + EPS) * g * out_scale


def _df_prompt_body(lam_ref, gs_ref, q_ref, k_ref, v_ref, o_ref, acc_ref, *stats, blk, scale, lam_init, shifted):
    qi = pl.program_id(2)
    lane = lax.broadcasted_iota(jnp.int32, (1, LANES), 1)
    q = q_ref[...] * (scale * LOG2_E)
    qm = [jnp.where((lane >= i * DF_QK) & (lane < (i + 1) * DF_QK), q, 0.0).astype(BF16) for i in range(4)]
    r = lax.broadcasted_iota(jnp.int32, (blk, blk), 0)
    c = lax.broadcasted_iota(jnp.int32, (blk, blk), 1)
    causal = c <= r
    if shifted:
        m_ref, l_ref = stats

    def tile_raw(j, diag):
        start = pl.multiple_of(j * blk, blk)
        k = k_ref[:, pl.ds(start, blk)].astype(BF16)
        v = v_ref[:, pl.ds(start, blk)].astype(BF16)
        feat = lax.broadcasted_iota(jnp.int32, (LANES, 1), 0)
        one = jnp.ones(v.shape, BF16)
        vh = [jnp.where(feat < HEAD, v, one), jnp.where(feat < HEAD, one, v)]
        ss = [jnp.dot(qm[i], k, preferred_element_type=F32) for i in range(4)]
        ps = [jnp.exp2(s) for s in ss]
        if diag:
            ps = [jnp.where(causal, p, 0.0) for p in ps]
        pvs = [lax.dot_general(ps[i].astype(BF16), vh[i // 2], _NT, preferred_element_type=F32)
               for i in range(4)]
        for i in range(4):
            if diag:
                acc_ref[i] = pvs[i]
            else:
                acc_ref[i] += pvs[i]

    def tile_shifted(j, diag):
        start = pl.multiple_of(j * blk, blk)
        k = k_ref[:, pl.ds(start, blk)].astype(BF16)
        v = v_ref[:, pl.ds(start, blk)].astype(BF16)
        for i in range(4):
            s = jnp.dot(qm[i], k, preferred_element_type=F32)
            if diag:
                s = jnp.where(causal, s, NEG_INF)
                m_new = jnp.max(s, axis=-1, keepdims=True)
                p = jnp.exp2(s - m_new)
                l_ref[i] = jnp.sum(p, axis=-1, keepdims=True)
                acc_ref[i] = lax.dot_general(p.astype(BF16), v, _NT, preferred_element_type=F32)
            else:
                m_old = m_ref[i]
                m_new = jnp.maximum(m_old, jnp.max(s, axis=-1, keepdims=True))
                a = jnp.exp2(m_old - m_new)
                p = jnp.exp2(s - m_new)
                l_ref[i] = a * l_ref[i] + jnp.sum(p, axis=-1, keepdims=True)
                acc_ref[i] = a * acc_ref[i] + lax.dot_general(p.astype(BF16), v, _NT,
                                                              preferred_element_type=F32)
            m_ref[i] = m_new

    tile = tile_shifted if shifted else tile_raw
    tile(qi, True)

    def body(i, carry):
        tile(qi - 1 - i, False)
        return carry

    lax.fori_loop(0, qi, body, 0)
    lam = _diff_lambda(lam_ref, lam_init)
    if shifted:
        den = [l_ref[i] for i in range(4)]
    else:
        den = [pltpu.roll(acc_ref[i], HEAD, 1) for i in range(4)]
    o0 = acc_ref[0] / den[0] - lam * (acc_ref[1] / den[1])
    o1 = acc_ref[2] / den[2] - lam * (acc_ref[3] / den[3])
    o = jnp.where(lane < HEAD, o0, o1)
    o_ref[...] = _head_rms(o, gs_ref[...], lane < HEAD, 1.0 - lam_init)


def _df_prompt(lam_rows, gs2, q, k, v, n_seq, lam_init, shifted):
    T = q.shape[0]
    S = T // n_seq
    blk = min(ATT_BLOCK, S)
    nq = S // blk
    pairs = MIX // LANES
    qspec = pl.BlockSpec((blk, LANES), lambda b, p, i: (b * nq + i, p))
    kspec = pl.BlockSpec((None, LANES, S), lambda b, p, i: (b, p, 0))
    stats = [pltpu.VMEM((4, blk, 1), F32)] * 2 if shifted else []
    return pl.pallas_call(
        functools.partial(_df_prompt_body, blk=blk, scale=DF_QK ** -0.5, lam_init=lam_init, shifted=shifted),
        grid=(n_seq, pairs, nq),
        in_specs=[_full(lam_rows.shape), _full((1, LANES)), qspec, kspec, kspec],
        out_specs=qspec,
        out_shape=jax.ShapeDtypeStruct((T, MIX), F32),
        scratch_shapes=[pltpu.VMEM((4, blk, LANES), F32)] + stats,
        compiler_params=_params(("parallel", "parallel", "arbitrary")),
        name="df_prompt_shifted" if shifted else "df_prompt",
    )(lam_rows, gs2, q, k, v)


def _merge_body(x_ref, g_ref, oa_ref, ob_ref, oc_ref, od_ref, wg_ref, wb_ref, wo_ref, out_ref):
    x = x_ref[...]
    D = x.shape[1]
    h = _rms(x, g_ref[...]).astype(BF16)
    acc = None
    for i, o_ref in enumerate((oa_ref, ob_ref, oc_ref, od_ref)):
        gate = jnp.dot(h, wg_ref[:, i * D:(i + 1) * D], preferred_element_type=F32)
        branch = jnp.dot(o_ref[...].astype(BF16), wb_ref[i], preferred_element_type=F32)
        t = branch / (1.0 + jnp.exp(-gate))
        acc = t if acc is None else acc + t
    out_ref[...] = x + jnp.dot(acc.astype(BF16), wo_ref[...], preferred_element_type=F32)


def _merge(x, g, oa, ob, oc, od, wg, wb, wo):
    T, D = x.shape
    tm = min(ROW_TILE, T)
    row = pl.BlockSpec((tm, D), lambda i: (i, 0))
    mix = pl.BlockSpec((tm, MIX), lambda i: (i, 0))
    return pl.pallas_call(
        _merge_body,
        grid=(T // tm,),
        in_specs=[row, _full((1, D)), mix, mix, mix, mix, _full(wg.shape), _full(wb.shape), _full(wo.shape)],
        out_specs=row,
        out_shape=jax.ShapeDtypeStruct((T, D), F32),
        compiler_params=_params(("parallel",)),
        name="merge",
    )(x, g, oa, ob, oc, od, wg, wb, wo)


def _memkv_body(m_ref, wt_ref, g_ref, bd_ref, k_ref, v_ref):
    yt = lax.dot_general(wt_ref[...], m_ref[...].astype(BF16), _NT, preferred_element_type=F32)
    k = yt[:MIX]
    ms = jnp.dot(bd_ref[...], (k * k).astype(BF16), preferred_element_type=F32)
    k_ref[...] = k * lax.rsqrt(ms + EPS) * g_ref[...]
    v_ref[...] = yt[MIX:]


def _memkv(mem, wt, g_col, bd64, n_seq):
    T, D = mem.shape
    M = T // n_seq
    out = pl.BlockSpec((None, MIX, M), lambda b: (b, 0, 0))
    return pl.pallas_call(
        _memkv_body,
        grid=(n_seq,),
        in_specs=[pl.BlockSpec((M, D), lambda b: (b, 0)), _full(wt.shape), _full((MIX, 1)), _full((MIX, MIX))],
        out_specs=[out, out],
        out_shape=[jax.ShapeDtypeStruct((n_seq, MIX, M), F32)] * 2,
        compiler_params=_params(("parallel",)),
        name="mem_kv",
    )(mem, wt, g_col, bd64)


def _cross_query(x, g, wq, gq, bd):
    h = _rms(x, g).astype(BF16)
    q = jnp.dot(h, wq, preferred_element_type=F32)
    return _group_rms(q, gq, bd) * HEAD ** -0.5


def _cross_heads(q, mk, mv):
    lane = lax.broadcasted_iota(jnp.int32, (1, LANES), 1)
    outs = []
    for p in range(MIX // LANES):
        qp = q[:, p * LANES:(p + 1) * LANES]
        kp = mk[p * LANES:(p + 1) * LANES, :].astype(BF16)
        vp = mv[p * LANES:(p + 1) * LANES, :].astype(BF16)
        oh = []
        for h in range(2):
            in_head = (lane >= h * HEAD) & (lane < (h + 1) * HEAD)
            s = jnp.dot(jnp.where(in_head, qp, 0.0).astype(BF16), kp, preferred_element_type=F32)
            e = jnp.exp(s - jnp.max(s, axis=-1, keepdims=True))
            o = lax.dot_general(e.astype(BF16), vp, _NT, preferred_element_type=F32)
            oh.append(o / jnp.sum(e, axis=-1, keepdims=True))
        outs.append(jnp.where(lane < HEAD, oh[0], oh[1]))
    return outs


def _cross_prompt_body(x_ref, g_ref, wq_ref, gq_ref, bd_ref, mk_ref, mv_ref, wo_ref, out_ref):
    x = x_ref[...]
    q = _cross_query(x, g_ref[...], wq_ref[...], gq_ref[...], bd_ref[...])
    o = jnp.concatenate(_cross_heads(q, mk_ref[...], mv_ref[...]), axis=-1)
    out_ref[...] = x + jnp.dot(o.astype(BF16), wo_ref[...], preferred_element_type=F32)


def _cross_prompt(x, g, wq, gq, bd64, mk, mv, wo, n_seq):
    T, D = x.shape
    S = T // n_seq
    M = mk.shape[2]
    tm = min(ROW_TILE, S)
    nt = S // tm
    row = pl.BlockSpec((tm, D), lambda b, i: (b * nt + i, 0))
    mem = pl.BlockSpec((None, MIX, M), lambda b, i: (b, 0, 0))
    return pl.pallas_call(
        _cross_prompt_body,
        grid=(n_seq, nt),
        in_specs=[row, _full((1, D)), _full(wq.shape), _full((1, MIX)), _full((MIX, MIX)), mem, mem,
                  _full(wo.shape)],
        out_specs=row,
        out_shape=jax.ShapeDtypeStruct((T, D), F32),
        compiler_params=_params(("parallel", "parallel")),
        name="cross_prompt",
    )(x, g, wq, gq, bd64, mk, mv, wo)


def _cross_sample_body(x_ref, g_ref, wq_ref, gq_ref, bd_ref, mk_ref, mv_ref, wo_ref, out_ref, q_scr, acc_scr,
                       *, nb, ts):
    j = pl.program_id(0)

    @pl.when(j == 0)
    def _():
        q_scr[...] = _cross_query(x_ref[...], g_ref[...], wq_ref[...], gq_ref[...], bd_ref[...])
        acc_scr[...] = jnp.zeros(acc_scr.shape, F32)

    rows = lax.broadcasted_iota(jnp.int32, (nb * ts, 1), 0)
    mine = rows == j
    for t in range(1, ts):
        mine = mine | (rows == t * nb + j)
    outs = _cross_heads(q_scr[...], mk_ref[...], mv_ref[...])
    for p, o in enumerate(outs):
        sl = slice(p * LANES, (p + 1) * LANES)
        acc_scr[:, sl] = jnp.where(mine, o, acc_scr[:, sl])

    @pl.when(j == nb - 1)
    def _():
        out_ref[...] = x_ref[...] + jnp.dot(acc_scr[...].astype(BF16), wo_ref[...], preferred_element_type=F32)


def _cross_sample(x, g, wq, gq, bd64, cache_k, cache_v, layer, wo, nb, ts):
    T, D = x.shape
    M = cache_k.shape[3]
    mem = pl.BlockSpec((None, None, MIX, M), lambda j: (layer, j, 0, 0))
    return pl.pallas_call(
        functools.partial(_cross_sample_body, nb=nb, ts=ts),
        grid=(nb,),
        in_specs=[_full((T, D)), _full((1, D)), _full(wq.shape), _full((1, MIX)), _full((MIX, MIX)), mem, mem,
                  _full(wo.shape)],
        out_specs=_full((T, D)),
        out_shape=jax.ShapeDtypeStruct((T, D), F32),
        scratch_shapes=[pltpu.VMEM((T, MIX), F32), pltpu.VMEM((T, MIX), F32)],
        compiler_params=_params(("arbitrary",)),
        name="cross_sample",
    )(x, g, wq, gq, bd64, cache_k, cache_v, wo)


def _conv_gate(a0, a1, a2, b, cw):
    ac = cw[3:4] + cw[0:1] * a2 + cw[1:2] * a1 + cw[2:3] * a0
    return ac / (1.0 + jnp.exp(-ac)) * b


def _ffn_prompt_body(x_ref, xh_ref, g_ref, wup_ref, cw_ref, wdn_ref, out_ref, cst_ref, h_scr, a_scr, b_scr,
                     acc_scr, *, tm, F, tf, tiles_per_seq):
    halo = BF16_ROWS
    first = (pl.program_id(0) % tiles_per_seq) == 0
    x = x_ref[...]
    h_scr[0:halo, :] = _rms(xh_ref[...], g_ref[...]).astype(BF16)
    h_scr[halo:halo + tm, :] = _rms(x, g_ref[...]).astype(BF16)
    n = F // tf

    def up(j, slot):
        a = jnp.dot(h_scr[...], wup_ref[:, j * tf:(j + 1) * tf], preferred_element_type=F32)
        a_scr[slot] = a
        a_scr[slot, 0:halo, :] = jnp.where(first, 0.0, a[0:halo])
        b_scr[slot] = jnp.dot(h_scr[halo:halo + tm, :], wup_ref[:, F + j * tf:F + (j + 1) * tf],
                              preferred_element_type=F32)

    up(0, 0)
    for j in range(n):
        slot = j % 2
        cols = slice(j * tf, (j + 1) * tf)
        if j + 1 < n:
            up(j + 1, 1 - slot)
        act = _conv_gate(a_scr[slot, halo:halo + tm, :], a_scr[slot, halo - 1:halo - 1 + tm, :],
                         a_scr[slot, halo - 2:halo - 2 + tm, :], b_scr[slot], cw_ref[:, cols])
        contrib = jnp.dot(act.astype(BF16), wdn_ref[cols, :], preferred_element_type=F32)
        if j == 0:
            acc_scr[...] = contrib
        else:
            acc_scr[...] += contrib
        cst_ref[0, :, cols] = a_scr[slot, tm:tm + halo, :]
    out_ref[...] = x + acc_scr[...]


def _ffn_prompt(x, g, wup, cw, wdn, n_seq):
    T, D = x.shape
    F = wdn.shape[0]
    S = T // n_seq
    tm = min(ROW_TILE, S)
    nt = S // tm
    halo = BF16_ROWS
    tf = FFN_CHUNK if F % FFN_CHUNK == 0 else LANES
    row = pl.BlockSpec((tm, D), lambda i: (i, 0))
    prev = pl.BlockSpec((halo, D), lambda i: (jnp.maximum(i * (tm // halo) - 1, 0), 0))
    return pl.pallas_call(
        functools.partial(_ffn_prompt_body, tm=tm, F=F, tf=tf, tiles_per_seq=nt),
        grid=(T // tm,),
        in_specs=[row, prev, _full((1, D)), _full(wup.shape), _full(cw.shape), _full(wdn.shape)],
        out_specs=[row, pl.BlockSpec((1, halo, F), lambda i: (i // nt, 0, 0))],
        out_shape=[jax.ShapeDtypeStruct((T, D), F32), jax.ShapeDtypeStruct((n_seq, halo, F), F32)],
        scratch_shapes=[pltpu.VMEM((halo + tm, D), BF16), pltpu.VMEM((2, halo + tm, tf), F32),
                        pltpu.VMEM((2, tm, tf), F32), pltpu.VMEM((tm, D), F32)],
        compiler_params=_params(("arbitrary",)),
        name="ffn_prompt",
    )(x, x, g, wup, cw, wdn)


def _ffn_sample_body(x_ref, prev_ref, g_ref, wup_ref, cw_ref, wdn_ref, out_ref, cst_ref, *, nb, F, tf):
    x = x_ref[...]
    T = x.shape[0]
    h = _rms(x, g_ref[...]).astype(BF16)
    acc = None
    for j in range(F // tf):
        cols = slice(j * tf, (j + 1) * tf)
        a = jnp.dot(h, wup_ref[:, cols], preferred_element_type=F32)
        b = jnp.dot(h, wup_ref[:, F + j * tf:F + (j + 1) * tf], preferred_element_type=F32)
        prev = prev_ref[:, cols]
        a1 = jnp.concatenate([prev[nb:2 * nb], a[:T - nb]], axis=0)
        a2 = jnp.concatenate([prev, a[:T - 2 * nb]], axis=0)
        act = _conv_gate(a, a1, a2, b, cw_ref[:, cols])
        contrib = jnp.dot(act.astype(BF16), wdn_ref[cols, :], preferred_element_type=F32)
        acc = contrib if acc is None else acc + contrib
        cst_ref[:, cols] = a[T - 2 * nb:]
    out_ref[...] = x + acc


def _ffn_sample(x, prev, g, wup, cw, wdn, nb):
    T, D = x.shape
    F = wdn.shape[0]
    tf = FFN_CHUNK if F % FFN_CHUNK == 0 else LANES
    return pl.pallas_call(
        functools.partial(_ffn_sample_body, nb=nb, F=F, tf=tf),
        grid=(1,),
        in_specs=[_full((T, D)), _full(prev.shape), _full((1, D)), _full(wup.shape), _full(cw.shape),
                  _full(wdn.shape)],
        out_specs=[_full((T, D)), _full(prev.shape)],
        out_shape=[jax.ShapeDtypeStruct((T, D), F32), jax.ShapeDtypeStruct(prev.shape, F32)],
        compiler_params=_params(("arbitrary",)),
        name="ffn_sample",
    )(x, prev, g, wup, cw, wdn)


def _local_sample_body(xb_ref, st_ref, uc_ref, vn_ref, wp_ref, sp_ref, coef_ref, bias_ref, ob_ref, oc_ref,
                       *, nb, ts, n_past):
    wl = _lane_window()
    ext = jnp.concatenate([st_ref[...], xb_ref[...]], axis=0)
    base = POOL_STATE * nb
    x = xb_ref[...]
    acc = x
    for k in range(1, POOL_STATE + 1):
        acc = acc + jnp.where(k < wl, ext[base - k * nb:base - k * nb + ts * nb], 0.0)
    for t in range(ts):
        rows = slice(t * nb, (t + 1) * nb)
        cnt = jnp.minimum(n_past + t + 1, wl).astype(F32)
        d = acc[rows] / cnt - x[rows]
        y = jnp.dot(d.astype(BF16), wp_ref[...], preferred_element_type=F32)
        ob_ref[rows, :] = y * sp_ref[...]
        mixed = bias_ref[t:t + 1, :]
        for s in range(t + 1):
            mixed = mixed + coef_ref[t * ts + s:t * ts + s + 1, :] * vn_ref[s * nb:(s + 1) * nb, :]
        oc_ref[rows, :] = uc_ref[rows, :] * mixed


def _local_sample(xb, state, uc, vn, wp_bd, sp, coef, bias, nb, ts, n_past):
    T = xb.shape[0]
    return pl.pallas_call(
        functools.partial(_local_sample_body, nb=nb, ts=ts, n_past=n_past),
        grid=(1,),
        in_specs=[_full((T, MIX)), _full(state.shape), _full((T, MIX)), _full((T, MIX)), _full((MIX, MIX)),
                  _full((1, MIX)), _full(coef.shape), _full(bias.shape)],
        out_specs=[_full((T, MIX))] * 2,
        out_shape=[jax.ShapeDtypeStruct((T, MIX), F32)] * 2,
        compiler_params=_params(("arbitrary",)),
        name="local_sample",
    )(xb, state, uc, vn, wp_bd, sp, coef, bias)


Q_PAD = 8


def _slot_queries(q8, slot_width):
    lane = lax.broadcasted_iota(jnp.int32, (1, MIX), 1)
    slots = [jnp.where((lane >= i * slot_width) & (lane < (i + 1) * slot_width), q8, 0.0)
             for i in range(MIX // slot_width)]
    return jnp.concatenate(slots, axis=0).astype(BF16)


def _segment(page_refs, s, pages_per_seg):
    parts = [page_refs[s * pages_per_seg + g][...].astype(BF16) for g in range(pages_per_seg)]
    return parts[0] if pages_per_seg == 1 else jnp.concatenate(parts, axis=1)


def _sb_decode_body(pt_ref, q_ref, kn_ref, vn_ref, *rest, G, scale):
    k_refs, v_refs = rest[:G], rest[G:2 * G]
    o_ref, acc_ref, car_ref = rest[2 * G:]
    j = pl.program_id(1)
    rows = (MIX // HEAD) * Q_PAD
    qbd = _slot_queries(q_ref[0] * (scale * LOG2_E), HEAD)
    page = k_refs[0].shape[1]
    seg = max(ATT_BLOCK, page)
    r = lax.broadcasted_iota(jnp.int32, (seg, seg), 0)
    c = lax.broadcasted_iota(jnp.int32, (seg, seg), 1)
    upper = (r > c).astype(BF16)

    def chain(kb, vb, up, valid):
        z2 = jnp.dot(qbd, kb, preferred_element_type=F32)
        ls, lk = _log2_sigmoids(z2)
        if valid is not None:
            lk = jnp.where(valid, lk, 0.0)
        bt = jnp.dot(lk.astype(BF16), up, preferred_element_type=F32)
        w = jnp.exp2(ls + bt)
        if valid is not None:
            w = jnp.where(valid, w, 0.0)
        pv = lax.dot_general(w.astype(BF16), vb, _NT, preferred_element_type=F32)
        return pv, bt[:, 0:1] + lk[:, 0:1]

    def fold(parts):
        acc, car = acc_ref[...], car_ref[...]
        for pv, tot in parts:
            acc = acc + jnp.exp2(car) * pv
            car = car + tot
        acc_ref[...] = acc
        car_ref[...] = car

    @pl.when(j == 0)
    def _():
        acc_ref[...] = jnp.zeros(acc_ref.shape, F32)
        car_ref[...] = jnp.zeros(car_ref.shape, F32)
        n = kn_ref.shape[2]
        t = lax.broadcasted_iota(jnp.int32, (rows, n), 0) & (Q_PAD - 1)
        s = lax.broadcasted_iota(jnp.int32, (rows, n), 1)
        fold([chain(kn_ref[0].astype(BF16), vn_ref[0].astype(BF16), upper[:n, :n], s < t)])

    pps = seg // page
    fold([chain(_segment(k_refs, s, pps), _segment(v_refs, s, pps), upper, None)
          for s in range(G // pps - 1, -1, -1)])

    @pl.when(j == pl.num_programs(1) - 1)
    def _():
        lane = lax.broadcasted_iota(jnp.int32, (1, MIX), 1)
        o = jnp.zeros((Q_PAD, MIX), F32)
        for h in range(MIX // HEAD):
            in_head = (lane >= h * HEAD) & (lane < (h + 1) * HEAD)
            o = jnp.where(in_head, acc_ref[h * Q_PAD:(h + 1) * Q_PAD, :], o)
        o_ref[0] = o


def _df_decode_body(pt_ref, lam_ref, gs_ref, q_ref, kn_ref, vn_ref, *rest, G, scale, lam_init):
    k_refs, v_refs = rest[:G], rest[G:2 * G]
    o_ref, acc_ref, m_ref, l_ref = rest[2 * G:]
    j = pl.program_id(1)
    slots = MIX // DF_QK
    rows = slots * Q_PAD
    qbd = _slot_queries(q_ref[0] * (scale * LOG2_E), DF_QK)
    page = k_refs[0].shape[1]
    seg = max(ATT_BLOCK, page)

    def block(kbs, vbs, valid):
        ss = [jnp.dot(qbd, kb, preferred_element_type=F32) for kb in kbs]
        if valid is not None:
            ss = [jnp.where(valid, s, NEG_INF) for s in ss]
        top = ss[0]
        for s in ss[1:]:
            top = jnp.maximum(top, s)
        m_old = m_ref[...]
        m_new = jnp.maximum(m_old, jnp.max(top, axis=-1, keepdims=True))
        a = jnp.exp2(m_old - m_new)
        ps = [jnp.exp2(s - m_new) for s in ss]
        psum = ps[0]
        for p in ps[1:]:
            psum = psum + p
        pv = None
        for p, vb in zip(ps, vbs):
            d = lax.dot_general(p.astype(BF16), vb, _NT, preferred_element_type=F32)
            pv = d if pv is None else pv + d
        l_ref[...] = a * l_ref[...] + jnp.sum(psum, axis=-1, keepdims=True)
        acc_ref[...] = a * acc_ref[...] + pv
        m_ref[...] = m_new

    @pl.when(j == 0)
    def _():
        acc_ref[...] = jnp.zeros(acc_ref.shape, F32)
        l_ref[...] = jnp.zeros(l_ref.shape, F32)
        m_ref[...] = jnp.full(m_ref.shape, NEG_INF, F32)
        n = kn_ref.shape[2]
        t = lax.broadcasted_iota(jnp.int32, (rows, n), 0) & (Q_PAD - 1)
        s = lax.broadcasted_iota(jnp.int32, (rows, n), 1)
        block([kn_ref[0].astype(BF16)], [vn_ref[0].astype(BF16)], s <= t)

    pps = seg // page
    nseg = G // pps
    block([_segment(k_refs, s, pps) for s in range(nseg)], [_segment(v_refs, s, pps) for s in range(nseg)], None)

    @pl.when(j == pl.num_programs(1) - 1)
    def _():
        lane = lax.broadcasted_iota(jnp.int32, (1, MIX), 1)
        lam = _diff_lambda(lam_ref, lam_init)
        norm = acc_ref[...] / l_ref[...]
        o = jnp.zeros((Q_PAD, MIX), F32)
        for h in range(MIX // HEAD):
            in_head = (lane >= h * HEAD) & (lane < (h + 1) * HEAD)
            r0 = (2 * h) * Q_PAD
            oh = norm[r0:r0 + Q_PAD, :] - lam * norm[r0 + Q_PAD:r0 + 2 * Q_PAD, :]
            o = jnp.where(in_head, oh, o)
        lane2 = lax.broadcasted_iota(jnp.int32, (1, LANES), 1)
        parts = [_head_rms(o[:, p * LANES:(p + 1) * LANES], gs_ref[...], lane2 < HEAD, 1.0 - lam_init)
                 for p in range(MIX // LANES)]
        o_ref[0] = jnp.concatenate(parts, axis=-1)


def _decode_specs(cache_k, layer, nb, n_pages, G):
    nj = n_pages // G
    page = cache_k.shape[3]

    def page_spec(g, reverse):
        def index(b, j, pt):
            jj = nj - 1 - j if reverse else j
            return (layer, pt[b, jj * G + g], 0, 0)
        return pl.BlockSpec((None, None, MIX, page), index)

    tok = pl.BlockSpec((1, Q_PAD, MIX), lambda b, j, pt: (b, 0, 0))
    new = pl.BlockSpec((1, MIX, LANES), lambda b, j, pt: (b, 0, 0))
    return nj, page, page_spec, tok, new


def _sb_decode(page_table, q, k_new, v_new, cache_k, cache_v, layer):
    nb, n_pages = page_table.shape
    G = PAGES_PER_STEP
    nj, page, page_spec, tok, new = _decode_specs(cache_k, layer, nb, n_pages, G)
    rows = (MIX // HEAD) * Q_PAD
    grid_spec = pltpu.PrefetchScalarGridSpec(
        num_scalar_prefetch=1,
        grid=(nb, nj),
        in_specs=[tok, new, new] + [page_spec(g, True) for g in range(G)] * 2,
        out_specs=tok,
        scratch_shapes=[pltpu.VMEM((rows, MIX), F32), pltpu.VMEM((rows, 1), F32)],
    )
    return pl.pallas_call(
        functools.partial(_sb_decode_body, G=G, scale=HEAD ** -0.5),
        grid_spec=grid_spec,
        out_shape=jax.ShapeDtypeStruct((nb, Q_PAD, MIX), F32),
        compiler_params=_params(("parallel", "arbitrary")),
        name="sb_decode",
    )(page_table, q, k_new, v_new, *([cache_k] * G), *([cache_v] * G))


def _df_decode(page_table, lam_rows, gs2, q, k_new, v_new, cache_k, cache_v, layer, lam_init):
    nb, n_pages = page_table.shape
    G = PAGES_PER_STEP
    nj, page, page_spec, tok, new = _decode_specs(cache_k, layer, nb, n_pages, G)
    rows = (MIX // DF_QK) * Q_PAD
    const = lambda shape: pl.BlockSpec(shape, lambda b, j, pt: (0,) * len(shape))
    grid_spec = pltpu.PrefetchScalarGridSpec(
        num_scalar_prefetch=1,
        grid=(nb, nj),
        in_specs=[const(lam_rows.shape), const((1, LANES)), tok, new, new]
        + [page_spec(g, False) for g in range(G)] * 2,
        out_specs=tok,
        scratch_shapes=[pltpu.VMEM((rows, MIX), F32), pltpu.VMEM((rows, 1), F32), pltpu.VMEM((rows, 1), F32)],
    )
    return pl.pallas_call(
        functools.partial(_df_decode_body, G=G, scale=DF_QK ** -0.5, lam_init=lam_init),
        grid_spec=grid_spec,
        out_shape=jax.ShapeDtypeStruct((nb, Q_PAD, MIX), F32),
        compiler_params=_params(("parallel", "arbitrary")),
        name="df_decode",
    )(page_table, lam_rows, gs2, q, k_new, v_new, *([cache_k] * G), *([cache_v] * G))


def _row(v):
    return v.reshape(1, -1).astype(F32)


def _tile_row(v, width):
    return jnp.tile(v.astype(F32), width // v.shape[0]).reshape(1, width)


def _pad_rows(a, rows):
    return jnp.pad(a, ((0, rows - a.shape[0]),) + ((0, 0),) * (a.ndim - 1))


def _block_diag(w):
    G, c, d = w.shape
    eye = jnp.eye(G, dtype=w.dtype)
    return (eye[:, None, :, None] * w[:, :, None, :]).reshape(G * c, G * d)


def kernel(x_prompt, x_sample, mem_prompt, cache_sb_k, cache_sb_v, cache_df_k, cache_df_v, cache_mem_k, cache_mem_v, state_pool, state_ffn_conv, page_table, g_mix, w_in, w_branch, w_out, w_pool, s_pool, g_gm, w_spatial, b_spatial, g_qd, g_kd, lam_q1, lam_k1, lam_q2, lam_k2, g_subln, g_cross, w_cq, w_ckv, w_co, g_cq, g_ck, g_ffn, w_up, w_conv, b_conv, w_down):
    B, S, D = x_prompt.shape
    NB, TS, _ = x_sample.shape
    depth = w_in.shape[0]
    M = mem_prompt.shape[1]
    F = w_down.shape[1]
    page = cache_sb_k.shape[2]
    n_pages = page_table.shape[1]
    n_past = n_pages * page
    heads = MIX // HEAD
    assert TS >= CONV_W - 1 and TS <= Q_PAD and TS <= GM_CHUNK and n_pages % PAGES_PER_STEP == 0

    bd32 = _block_diag_mean(MIX, DF_QK)
    bd64 = _block_diag_mean(MIX, HEAD)
    feat_major = lambda c: c.transpose(0, 1, 3, 4, 2).reshape(c.shape[0], c.shape[1], MIX, c.shape[2])
    csb_k, csb_v, cdf_k, cdf_v = (feat_major(c) for c in (cache_sb_k, cache_sb_v, cache_df_k, cache_df_v))
    cmem_k, cmem_v = feat_major(cache_mem_k), feat_major(cache_mem_v)
    tok_major = lambda a: a.reshape(a.shape[0], heads, HEAD, a.shape[2]).transpose(0, 3, 1, 2)

    xp = x_prompt.reshape(B * S, D)
    xs = x_sample.transpose(1, 0, 2).reshape(TS * NB, D)
    mem = mem_prompt.reshape(B * M, D)

    to_batch = lambda a: a.reshape(TS, NB, MIX).transpose(1, 0, 2)
    to_time = lambda a: a[:, :TS].transpose(1, 0, 2).reshape(TS * NB, MIX)
    pad_tok = lambda a, n: jnp.pad(to_batch(a), ((0, 0), (0, n - TS), (0, 0)))
    new_page = lambda a: jnp.pad(to_batch(a).transpose(0, 2, 1), ((0, 0), (0, 0), (0, LANES - TS)))
    kv_cols = lambda w: jnp.concatenate([w[:, i * MIX:(i + 1) * MIX] for i in (1, 2, 7, 8)], axis=1)
    tok_cols = lambda w: jnp.concatenate([w[:, i * MIX:(i + 1) * MIX] for i in (0, 3, 4, 5, 6)], axis=1)

    outs = [[] for _ in range(15)]
    for l in range(depth):
        lam_init = 0.8 - 0.6 * math.exp(-0.3 * l)
        w_l = w_in[l]
        wtok = tok_cols(w_l).astype(BF16)
        wkv = kv_cols(w_l).astype(BF16)
        wkv_t = wkv.T
        wg = w_l[:, 9 * MIX:].astype(BF16)
        wb = w_branch[l].astype(BF16)
        wo = w_out[l].astype(BF16)
        gq, gk = _tile_row(g_qd[l], MIX), _tile_row(g_kd[l], MIX)
        gk_col = gk.reshape(MIX, 1)
        ggm, gmix = _row(g_gm[l]), _row(g_mix[l])
        wp_bd = _block_diag(w_pool[l]).astype(BF16)
        sp = _row(s_pool[l])
        bs_full = jnp.repeat(b_spatial[l].T, MIX // GM_GROUPS, axis=1).astype(F32)
        lam_rows = _pad_rows(jnp.stack([lam_q1[l], lam_k1[l], lam_q2[l], lam_k2[l]]).astype(F32), 8)
        gs2 = _tile_row(g_subln[l], LANES)
        wcq = w_cq[l].astype(BF16)
        wckv_t = w_ckv[l].astype(BF16).T
        wco = w_co[l].astype(BF16)
        gcq, gck_col = _tile_row(g_cq[l], MIX), _tile_row(g_ck[l], MIX).reshape(MIX, 1)
        wup = w_up[l].astype(BF16)
        wdn = w_down[l].astype(BF16)
        cw = _pad_rows(jnp.concatenate([w_conv[l], b_conv[l][None]], axis=0).astype(F32), 8)

        qa, xb, uc, vn, qd, ka, va, kd, vd = _proj_in(xp, gmix, wtok, wkv_t, gq, gk_col, ggm, bd32, B, True)
        oa = _sb_prompt(qa, ka, va, B)
        ob, oc = _local_prompt(xb, uc, vn, wp_bd, sp, w_spatial[l].astype(F32), bs_full, B)
        raw_exp_ok = (DF_QK ** 0.5 * 1.01 * jnp.max(jnp.abs(g_qd[l])) * jnp.max(jnp.abs(g_kd[l]))
                      <= DF_RAW_EXP_BOUND)
        od = lax.cond(raw_exp_ok,
                      functools.partial(_df_prompt, n_seq=B, lam_init=lam_init, shifted=False),
                      functools.partial(_df_prompt, n_seq=B, lam_init=lam_init, shifted=True),
                      lam_rows, gs2, qd, kd, vd)
        xp = _merge(xp, gmix, oa, ob, oc, od, wg, wb, wo)
        mk, mv = _memkv(mem, wckv_t, gck_col, bd64, B)
        xp = _cross_prompt(xp, _row(g_cross[l]), wcq, gcq, bd64, mk, mv, wco, B)
        xp, cst = _ffn_prompt(xp, _row(g_ffn[l]), wup, cw, wdn, B)
        outs[0].append(tok_major(ka))
        outs[1].append(tok_major(va))
        outs[2].append(tok_major(kd))
        outs[3].append(tok_major(vd))
        outs[4].append(tok_major(mk))
        outs[5].append(tok_major(mv))
        outs[6].append(xb.reshape(B, S, MIX)[:, S - POOL_STATE:])
        outs[7].append(cst[:, -(CONV_W - 1):])

        qa, xb, uc, vn, qd, ka, va, kd, vd = _proj_in(xs, gmix, wtok, wkv, gq, gk, ggm, bd32, 1, False)
        oa = _sb_decode(page_table, pad_tok(qa, Q_PAD), new_page(ka), new_page(va), csb_k, csb_v, l)
        od = _df_decode(page_table, lam_rows, gs2, pad_tok(qd, Q_PAD), new_page(kd), new_page(vd),
                        cdf_k, cdf_v, l, lam_init)
        state_t = state_pool[l].astype(F32).transpose(1, 0, 2).reshape(POOL_STATE * NB, MIX)
        ws_t = jnp.where(jnp.tril(jnp.ones((TS, TS), bool)), w_spatial[l][:, :TS, :TS], 0.0)
        coef = _pad_rows(jnp.repeat(ws_t.transpose(1, 2, 0).reshape(TS * TS, GM_GROUPS),
                                    MIX // GM_GROUPS, axis=1).astype(F32), -(-TS * TS // 8) * 8)
        bias = _pad_rows(bs_full[:TS], 8)
        ob, oc = _local_sample(xb, state_t, uc, vn, wp_bd, sp, coef, bias, NB, TS, n_past)
        xs = _merge(xs, gmix, to_time(oa), ob, oc, to_time(od), wg, wb, wo)
        xs = _cross_sample(xs, _row(g_cross[l]), wcq, gcq, bd64, cmem_k, cmem_v, l, wco, NB, TS)
        prev_t = state_ffn_conv[l].astype(F32).transpose(1, 0, 2).reshape((CONV_W - 1) * NB, F)
        xs, cst_s = _ffn_sample(xs, prev_t, _row(g_ffn[l]), wup, cw, wdn, NB)
        heads_s = lambda a: to_batch(a).reshape(NB, TS, heads, HEAD)
        outs[8].append(heads_s(ka))
        outs[9].append(heads_s(va))
        outs[10].append(heads_s(kd))
        outs[11].append(heads_s(vd))
        pool_ext = jnp.concatenate([state_pool[l].astype(F32), to_batch(xb)], axis=1)
        outs[12].append(pool_ext[:, -POOL_STATE:])
        outs[13].append(cst_s.reshape(CONV_W - 1, NB, F).transpose(1, 0, 2))
        outs[14].append(to_batch(vn))

    y_prompt = xp.reshape(B, S, D)
    y_sample = xs.reshape(TS, NB, D).transpose(1, 0, 2)
    return (y_prompt, y_sample) + tuple(jnp.stack(o) for o in outs)
```

```python
import functools
import math

import numpy as np
import jax
import jax.numpy as jnp
from jax import lax
from jax.experimental import pallas as pl
from jax.experimental.pallas import tpu as pltpu

F32 = jnp.float32
BF16 = jnp.bfloat16

EPS = 1e-6
NEG_INF = -1e30
LOG2_E = math.log2(math.e)
MIX = 256
HEAD = 64
DF_QK = 32
POOL_WINDOWS = (2, 4, 8, 16)
POOL_STATE = max(POOL_WINDOWS) - 1
POOL_PAD = 16
GM_CHUNK = 128
GM_GROUPS = 4
CONV_W = 3
LANES = 128
BF16_ROWS = 16
ROW_TILE = 512
ATT_BLOCK = 256
FFN_CHUNK = 256
PAGES_PER_STEP = 16
DF_RAW_EXP_BOUND = 40.0
VMEM_LIMIT = 56 * 1024 * 1024

_NT = (((1,), (1,)), ((), ()))


def _params(sem):
    return pltpu.CompilerParams(dimension_semantics=sem, vmem_limit_bytes=VMEM_LIMIT)


def _rms(x, g):
    return x * lax.rsqrt(jnp.mean(x * x, axis=-1, keepdims=True) + EPS) * g


def _group_rms(v, g, bd):
    ms = jnp.dot((v * v).astype(BF16), bd, preferred_element_type=F32)
    return v * lax.rsqrt(ms + EPS) * g


def _block_diag_mean(width, group):
    i = np.arange(width)
    return jnp.asarray((i[:, None] // group == i[None, :] // group) / group, BF16)


def _full(shape):
    return pl.BlockSpec(shape, lambda *_: (0,) * len(shape))


def _lane_window():
    lane = lax.broadcasted_iota(jnp.int32, (1, MIX), 1)
    gdim = MIX // len(POOL_WINDOWS)
    wl = jnp.full((1, MIX), POOL_WINDOWS[-1], jnp.int32)
    for g in range(len(POOL_WINDOWS) - 2, -1, -1):
        wl = jnp.where(lane < (g + 1) * gdim, POOL_WINDOWS[g], wl)
    return wl


def _proj_in_body(x_ref, g_ref, wtok_ref, wkv_ref, gq_ref, gk_ref, ggm_ref, bd_ref,
                  qa_ref, xb_ref, uc_ref, vn_ref, qd_ref, ka_ref, va_ref, kd_ref, vd_ref, *, kv_major):
    h = _rms(x_ref[...], g_ref[...]).astype(BF16)
    y = jnp.dot(h, wtok_ref[...], preferred_element_type=F32)
    part = lambda i: y[:, i * MIX:(i + 1) * MIX]
    qa_ref[...] = part(0)
    xb_ref[...] = part(1)
    uc_ref[...] = part(2)
    vn_ref[...] = _rms(part(3), ggm_ref[...])
    qd_ref[...] = _group_rms(part(4), gq_ref[...], bd_ref[...])
    if kv_major:
        yt = lax.dot_general(wkv_ref[...], h, _NT, preferred_element_type=F32)
        kd = yt[2 * MIX:3 * MIX]
        ms = jnp.dot(bd_ref[...], (kd * kd).astype(BF16), preferred_element_type=F32)
        ka_ref[...] = yt[0:MIX]
        va_ref[...] = yt[MIX:2 * MIX]
        kd_ref[...] = kd * lax.rsqrt(ms + EPS) * gk_ref[...]
        vd_ref[...] = yt[3 * MIX:4 * MIX]
    else:
        y2 = jnp.dot(h, wkv_ref[...], preferred_element_type=F32)
        ka_ref[...] = y2[:, 0:MIX]
        va_ref[...] = y2[:, MIX:2 * MIX]
        kd_ref[...] = _group_rms(y2[:, 2 * MIX:3 * MIX], gk_ref[...], bd_ref[...])
        vd_ref[...] = y2[:, 3 * MIX:4 * MIX]


def _proj_in(x, g, wtok, wkv, gq, gk, ggm, bd32, n_seq, kv_major):
    T, D = x.shape
    S = T // n_seq
    tm = min(ROW_TILE, S)
    nt = S // tm
    row = pl.BlockSpec((tm, D), lambda i: (i, 0))
    out = pl.BlockSpec((tm, MIX), lambda i: (i, 0))
    tok_shape = jax.ShapeDtypeStruct((T, MIX), F32)
    if kv_major:
        kv_spec = pl.BlockSpec((None, MIX, tm), lambda i: (i // nt, 0, i % nt))
        kv_shape = jax.ShapeDtypeStruct((n_seq, MIX, S), F32)
    else:
        kv_spec, kv_shape = out, tok_shape
    return pl.pallas_call(
        functools.partial(_proj_in_body, kv_major=kv_major),
        grid=(T // tm,),
        in_specs=[row, _full((1, D)), _full(wtok.shape), _full(wkv.shape), _full((1, MIX)), _full(gk.shape),
                  _full((1, MIX)), _full((MIX, MIX))],
        out_specs=[out] * 5 + [kv_spec] * 4,
        out_shape=[tok_shape] * 5 + [kv_shape] * 4,
        compiler_params=_params(("parallel",)),
        name="proj_in",
    )(x, g, wtok, wkv, gq, gk, ggm, bd32)


def _local_prompt_body(xb_ref, uc_ref, vn_ref, wp_ref, sp_ref, ws_ref, bs_ref, ob_ref, oc_ref, ext_ref):
    S = xb_ref.shape[0]
    ext_ref[0:POOL_PAD, :] = jnp.zeros((POOL_PAD, MIX), F32)
    ext_ref[POOL_PAD:POOL_PAD + S, :] = xb_ref[...]
    wl = _lane_window()
    lane = lax.broadcasted_iota(jnp.int32, (1, MIX), 1)
    R = min(256, S)
    for c in range(S // R):
        x = xb_ref[c * R:(c + 1) * R, :]
        acc = x
        for k in range(1, POOL_STATE + 1):
            lo = POOL_PAD + c * R - k
            acc = acc + jnp.where(k < wl, ext_ref[lo:lo + R, :], 0.0)
        pos = c * R + lax.broadcasted_iota(jnp.int32, (R, 1), 0)
        cnt = jnp.minimum(pos + 1, wl).astype(F32)
        d = acc / cnt - x
        y = jnp.dot(d.astype(BF16), wp_ref[...], preferred_element_type=F32)
        ob_ref[c * R:(c + 1) * R, :] = y * sp_ref[...]

    C = min(GM_CHUNK, S)
    r = lax.broadcasted_iota(jnp.int32, (C, C), 0)
    cidx = lax.broadcasted_iota(jnp.int32, (C, C), 1)
    gdim = MIX // GM_GROUPS
    ws = [jnp.where(r >= cidx, ws_ref[g][:C, :C], 0.0).astype(BF16) for g in range(GM_GROUPS)]
    bias = bs_ref[0:C, :]
    for n in range(S // C):
        vn = vn_ref[n * C:(n + 1) * C, :].astype(BF16)
        mixed = jnp.dot(ws[0], vn, preferred_element_type=F32)
        for g in range(1, GM_GROUPS):
            mixed = jnp.where(lane >= g * gdim, jnp.dot(ws[g], vn, preferred_element_type=F32), mixed)
        oc_ref[n * C:(n + 1) * C, :] = uc_ref[n * C:(n + 1) * C, :] * (mixed + bias)


def _local_prompt(xb, uc, vn, wp_bd, sp, ws, bs_full, n_seq):
    T = xb.shape[0]
    S = T // n_seq
    seq = pl.BlockSpec((S, MIX), lambda b: (b, 0))
    return pl.pallas_call(
        _local_prompt_body,
        grid=(n_seq,),
        in_specs=[seq, seq, seq, _full((MIX, MIX)), _full((1, MIX)), _full(ws.shape), _full(bs_full.shape)],
        out_specs=[seq, seq],
        out_shape=[jax.ShapeDtypeStruct((T, MIX), F32)] * 2,
        scratch_shapes=[pltpu.VMEM((POOL_PAD + S, MIX), F32)],
        compiler_params=_params(("parallel",)),
        name="local_prompt",
    )(xb, uc, vn, wp_bd, sp, ws, bs_full)


def _log_sigmoids(z):
    ls = jnp.minimum(z, 0.0) - jnp.log(1.0 + jnp.exp(-jnp.abs(z)))
    return ls, ls - z


def _suffix_sums(lk, upper):
    hi = lk.astype(BF16)
    lo = (lk - hi.astype(F32)).astype(BF16)
    return (jnp.dot(hi, upper, preferred_element_type=F32)
            + jnp.dot(lo, upper, preferred_element_type=F32))


def _log2_sigmoids(z2):
    ls2 = jnp.minimum(z2, 0.0) - jnp.log2(1.0 + jnp.exp2(-jnp.abs(z2)))
    return ls2, ls2 - z2


def _sweep_blocks(qi, scores, tile, bufs):
    a, b = bufs
    scores(qi, a)
    tile(qi, a, b, True)

    def pair(p, carry):
        j = qi - 1 - 2 * p
        tile(j, b, a, False)
        tile(j - 1, a, b, False)
        return carry

    lax.fori_loop(0, qi // 2, pair, 0)

    @pl.when(qi % 2 == 1)
    def _():
        tile(0, b, a, False)


def _sb_prompt_body(q_ref, k_ref, v_ref, o_ref, acc_ref, car_ref, za_ref, zb_ref, *, blk, scale):
    qi = pl.program_id(2)
    lane = lax.broadcasted_iota(jnp.int32, (1, LANES), 1)
    q = q_ref[...] * (scale * LOG2_E)
    qh = [jnp.where(lane < HEAD, q, 0.0).astype(BF16), jnp.where(lane >= HEAD, q, 0.0).astype(BF16)]
    r = lax.broadcasted_iota(jnp.int32, (blk, blk), 0)
    c = lax.broadcasted_iota(jnp.int32, (blk, blk), 1)
    upper = (r > c).astype(BF16)
    causal = c < r

    heads = range(2)

    def scores(j, buf):
        start = pl.multiple_of(j * blk, blk)
        k = k_ref[:, pl.ds(start, blk)].astype(BF16)
        for h in heads:
            buf[h] = jnp.dot(qh[h], k, preferred_element_type=F32)

    def tile(j, cur, nxt, diag):
        scores(jnp.maximum(j - 1, 0), nxt)
        start = pl.multiple_of(j * blk, blk)
        v = v_ref[:, pl.ds(start, blk)].astype(BF16)
        zs = [cur[h] for h in heads]
        lss, lks = zip(*[_log2_sigmoids(z) for z in zs])
        if diag:
            lks = [jnp.where(causal, lk, 0.0) for lk in lks]
        bts = [jnp.dot(lks[h].astype(BF16), upper, preferred_element_type=F32) for h in heads]
        ws = [jnp.exp2(lss[h] + bts[h]) for h in heads]
        if diag:
            ws = [jnp.where(causal, w, 0.0) for w in ws]
        pvs = [lax.dot_general(ws[h].astype(BF16), v, _NT, preferred_element_type=F32) for h in heads]
        for h in heads:
            tot = bts[h][:, 0:1] + lks[h][:, 0:1]
            if diag:
                acc_ref[h] = pvs[h]
                car_ref[h] = tot
            else:
                acc_ref[h] += jnp.exp2(car_ref[h]) * pvs[h]
                car_ref[h] += tot

    _sweep_blocks(qi, scores, tile, (za_ref, zb_ref))
    o_ref[...] = jnp.where(lane < HEAD, acc_ref[0], acc_ref[1])


def _sb_prompt(q, k, v, n_seq):
    T = q.shape[0]
    S = T // n_seq
    blk = min(ATT_BLOCK, S)
    nq = S // blk
    pairs = MIX // LANES
    qspec = pl.BlockSpec((blk, LANES), lambda b, p, i: (b * nq + i, p))
    kspec = pl.BlockSpec((None, LANES, S), lambda b, p, i: (b, p, 0))
    return pl.pallas_call(
        functools.partial(_sb_prompt_body, blk=blk, scale=HEAD ** -0.5),
        grid=(n_seq, pairs, nq),
        in_specs=[qspec, kspec, kspec],
        out_specs=qspec,
        out_shape=jax.ShapeDtypeStruct((T, MIX), F32),
        scratch_shapes=[pltpu.VMEM((2, blk, LANES), F32), pltpu.VMEM((2, blk, 1), F32),
                        pltpu.VMEM((2, blk, blk), F32), pltpu.VMEM((2, blk, blk), F32)],
        compiler_params=_params(("parallel", "parallel", "arbitrary")),
        name="sb_prompt",
    )(q, k, v)


def _diff_lambda(lam_ref, lam_init):
    s1 = jnp.sum(lam_ref[0:1, :] * lam_ref[1:2, :], axis=-1, keepdims=True)
    s2 = jnp.sum(lam_ref[2:3, :] * lam_ref[3:4, :], axis=-1, keepdims=True)
    return jnp.exp(s1) - jnp.exp(s2) + lam_init


def _head_rms(o, g, lane_in_head0, out_scale):
    o2 = o * o
    s0 = jnp.sum(jnp.where(lane_in_head0, o2, 0.0), axis=-1, keepdims=True)
    s1 = jnp.sum(jnp.where(lane_in_head0, 0.0, o2), axis=-1, keepdims=True)
    ms = jnp.where(lane_in_head0, s0, s1) * (1.0 / HEAD)
    return o * lax.rsqrt(ms + EPS) * g * out_scale


def _df_prompt_body(lam_ref, gs_ref, q_ref, k_ref, v_ref, o_ref, acc_ref, *stats, blk, scale, lam_init, shifted):
    qi = pl.program_id(2)
    lane = lax.broadcasted_iota(jnp.int32, (1, LANES), 1)
    q = q_ref[...] * (scale * LOG2_E)
    qm = [jnp.where((lane >= i * DF_QK) & (lane < (i + 1) * DF_QK), q, 0.0).astype(BF16) for i in range(4)]
    r = lax.broadcasted_iota(jnp.int32, (blk, blk), 0)
    c = lax.broadcasted_iota(jnp.int32, (blk, blk), 1)
    causal = c <= r
    if shifted:
        m_ref, l_ref = stats
        bufs = (None, None)
    else:
        bufs = stats

    def scores(j, buf):
        if shifted:
            return
        start = pl.multiple_of(j * blk, blk)
        k = k_ref[:, pl.ds(start, blk)].astype(BF16)
        for i in range(4):
            buf[i] = jnp.dot(qm[i], k, preferred_element_type=F32)

    def tile_raw(j, cur, nxt, diag):
        scores(jnp.maximum(j - 1, 0), nxt)
        start = pl.multiple_of(j * blk, blk)
        v = v_ref[:, pl.ds(start, blk)].astype(BF16)
        feat = lax.broadcasted_iota(jnp.int32, (LANES, 1), 0)
        one = jnp.ones(v.shape, BF16)
        vh = [jnp.where(feat < HEAD, v, one), jnp.where(feat < HEAD, one, v)]
        ss = [cur[i] for i in range(4)]
        ps = [jnp.exp2(s) for s in ss]
        if diag:
            ps = [jnp.where(causal, p, 0.0) for p in ps]
        pvs = [lax.dot_general(ps[i].astype(BF16), vh[i // 2], _NT, preferred_element_type=F32)
               for i in range(4)]
        for i in range(4):
            if diag:
                acc_ref[i] = pvs[i]
            else:
                acc_ref[i] += pvs[i]

    def tile_shifted(j, cur, nxt, diag):
        start = pl.multiple_of(j * blk, blk)
        k = k_ref[:, pl.ds(start, blk)].astype(BF16)
        v = v_ref[:, pl.ds(start, blk)].astype(BF16)
        for i in range(4):
            s = jnp.dot(qm[i], k, preferred_element_type=F32)
            if diag:
                s = jnp.where(causal, s, NEG_INF)
                m_new = jnp.max(s, axis=-1, keepdims=True)
                p = jnp.exp2(s - m_new)
                l_ref[i] = jnp.sum(p, axis=-1, keepdims=True)
                acc_ref[i] = lax.dot_general(p.astype(BF16), v, _NT, preferred_element_type=F32)
            else:
                m_old = m_ref[i]
                m_new = jnp.maximum(m_old, jnp.max(s, axis=-1, keepdims=True))
                a = jnp.exp2(m_old - m_new)
                p = jnp.exp2(s - m_new)
                l_ref[i] = a * l_ref[i] + jnp.sum(p, axis=-1, keepdims=True)
                acc_ref[i] = a * acc_ref[i] + lax.dot_general(p.astype(BF16), v, _NT,
                                                              preferred_element_type=F32)
            m_ref[i] = m_new

    _sweep_blocks(qi, scores, tile_shifted if shifted else tile_raw, bufs)
    lam = _diff_lambda(lam_ref, lam_init)
    if shifted:
        den = [l_ref[i] for i in range(4)]
    else:
        den = [pltpu.roll(acc_ref[i], HEAD, 1) for i in range(4)]
    o0 = acc_ref[0] / den[0] - lam * (acc_ref[1] / den[1])
    o1 = acc_ref[2] / den[2] - lam * (acc_ref[3] / den[3])
    o = jnp.where(lane < HEAD, o0, o1)
    o_ref[...] = _head_rms(o, gs_ref[...], lane < HEAD, 1.0 - lam_init)


def _df_prompt(lam_rows, gs2, q, k, v, n_seq, lam_init, shifted):
    T = q.shape[0]
    S = T // n_seq
    blk = min(ATT_BLOCK, S)
    nq = S // blk
    pairs = MIX // LANES
    qspec = pl.BlockSpec((blk, LANES), lambda b, p, i: (b * nq + i, p))
    kspec = pl.BlockSpec((None, LANES, S), lambda b, p, i: (b, p, 0))
    stats = [pltpu.VMEM((4, blk, 1), F32)] * 2 if shifted else [pltpu.VMEM((4, blk, blk), F32)] * 2
    return pl.pallas_call(
        functools.partial(_df_prompt_body, blk=blk, scale=DF_QK ** -0.5, lam_init=lam_init, shifted=shifted),
        grid=(n_seq, pairs, nq),
        in_specs=[_full(lam_rows.shape), _full((1, LANES)), qspec, kspec, kspec],
        out_specs=qspec,
        out_shape=jax.ShapeDtypeStruct((T, MIX), F32),
        scratch_shapes=[pltpu.VMEM((4, blk, LANES), F32)] + stats,
        compiler_params=_params(("parallel", "parallel", "arbitrary")),
        name="df_prompt_shifted" if shifted else "df_prompt",
    )(lam_rows, gs2, q, k, v)


def _merge_body(x_ref, g_ref, oa_ref, ob_ref, oc_ref, od_ref, wg_ref, wb_ref, wo_ref, out_ref):
    x = x_ref[...]
    D = x.shape[1]
    h = _rms(x, g_ref[...]).astype(BF16)
    acc = None
    for i, o_ref in enumerate((oa_ref, ob_ref, oc_ref, od_ref)):
        gate = jnp.dot(h, wg_ref[:, i * D:(i + 1) * D], preferred_element_type=F32)
        branch = jnp.dot(o_ref[...].astype(BF16), wb_ref[i], preferred_element_type=F32)
        t = branch / (1.0 + jnp.exp(-gate))
        acc = t if acc is None else acc + t
    out_ref[...] = x + jnp.dot(acc.astype(BF16), wo_ref[...], preferred_element_type=F32)


def _merge(x, g, oa, ob, oc, od, wg, wb, wo):
    T, D = x.shape
    tm = min(ROW_TILE, T)
    row = pl.BlockSpec((tm, D), lambda i: (i, 0))
    mix = pl.BlockSpec((tm, MIX), lambda i: (i, 0))
    return pl.pallas_call(
        _merge_body,
        grid=(T // tm,),
        in_specs=[row, _full((1, D)), mix, mix, mix, mix, _full(wg.shape), _full(wb.shape), _full(wo.shape)],
        out_specs=row,
        out_shape=jax.ShapeDtypeStruct((T, D), F32),
        compiler_params=_params(("parallel",)),
        name="merge",
    )(x, g, oa, ob, oc, od, wg, wb, wo)


def _memkv_body(m_ref, wt_ref, g_ref, bd_ref, k_ref, v_ref):
    yt = lax.dot_general(wt_ref[...], m_ref[...].astype(BF16), _NT, preferred_element_type=F32)
    k = yt[:MIX]
    ms = jnp.dot(bd_ref[...], (k * k).astype(BF16), preferred_element_type=F32)
    k_ref[...] = k * lax.rsqrt(ms + EPS) * g_ref[...]
    v_ref[...] = yt[MIX:]


def _memkv(mem, wt, g_col, bd64, n_seq):
    T, D = mem.shape
    M = T // n_seq
    out = pl.BlockSpec((None, MIX, M), lambda b: (b, 0, 0))
    return pl.pallas_call(
        _memkv_body,
        grid=(n_seq,),
        in_specs=[pl.BlockSpec((M, D), lambda b: (b, 0)), _full(wt.shape), _full((MIX, 1)), _full((MIX, MIX))],
        out_specs=[out, out],
        out_shape=[jax.ShapeDtypeStruct((n_seq, MIX, M), F32)] * 2,
        compiler_params=_params(("parallel",)),
        name="mem_kv",
    )(mem, wt, g_col, bd64)


def _cross_query(x, g, wq, gq, bd):
    h = _rms(x, g).astype(BF16)
    q = jnp.dot(h, wq, preferred_element_type=F32)
    return _group_rms(q, gq, bd) * HEAD ** -0.5


def _cross_heads(q, mk, mv):
    lane = lax.broadcasted_iota(jnp.int32, (1, LANES), 1)
    outs = []
    for p in range(MIX // LANES):
        qp = q[:, p * LANES:(p + 1) * LANES]
        kp = mk[p * LANES:(p + 1) * LANES, :].astype(BF16)
        vp = mv[p * LANES:(p + 1) * LANES, :].astype(BF16)
        oh = []
        for h in range(2):
            in_head = (lane >= h * HEAD) & (lane < (h + 1) * HEAD)
            s = jnp.dot(jnp.where(in_head, qp, 0.0).astype(BF16), kp, preferred_element_type=F32)
            e = jnp.exp(s - jnp.max(s, axis=-1, keepdims=True))
            o = lax.dot_general(e.astype(BF16), vp, _NT, preferred_element_type=F32)
            oh.append(o / jnp.sum(e, axis=-1, keepdims=True))
        outs.append(jnp.where(lane < HEAD, oh[0], oh[1]))
    return outs


def _cross_prompt_body(x_ref, g_ref, wq_ref, gq_ref, bd_ref, mk_ref, mv_ref, wo_ref, out_ref):
    x = x_ref[...]
    q = _cross_query(x, g_ref[...], wq_ref[...], gq_ref[...], bd_ref[...])
    o = jnp.concatenate(_cross_heads(q, mk_ref[...], mv_ref[...]), axis=-1)
    out_ref[...] = x + jnp.dot(o.astype(BF16), wo_ref[...], preferred_element_type=F32)


def _cross_prompt(x, g, wq, gq, bd64, mk, mv, wo, n_seq):
    T, D = x.shape
    S = T // n_seq
    M = mk.shape[2]
    tm = min(ROW_TILE, S)
    nt = S // tm
    row = pl.BlockSpec((tm, D), lambda b, i: (b * nt + i, 0))
    mem = pl.BlockSpec((None, MIX, M), lambda b, i: (b, 0, 0))
    return pl.pallas_call(
        _cross_prompt_body,
        grid=(n_seq, nt),
        in_specs=[row, _full((1, D)), _full(wq.shape), _full((1, MIX)), _full((MIX, MIX)), mem, mem,
                  _full(wo.shape)],
        out_specs=row,
        out_shape=jax.ShapeDtypeStruct((T, D), F32),
        compiler_params=_params(("parallel", "parallel")),
        name="cross_prompt",
    )(x, g, wq, gq, bd64, mk, mv, wo)


def _cross_sample_body(x_ref, g_ref, wq_ref, gq_ref, bd_ref, mk_ref, mv_ref, wo_ref, out_ref, q_scr, acc_scr,
                       *, nb, ts):
    j = pl.program_id(0)

    @pl.when(j == 0)
    def _():
        q_scr[...] = _cross_query(x_ref[...], g_ref[...], wq_ref[...], gq_ref[...], bd_ref[...])
        acc_scr[...] = jnp.zeros(acc_scr.shape, F32)

    rows = lax.broadcasted_iota(jnp.int32, (nb * ts, 1), 0)
    mine = rows == j
    for t in range(1, ts):
        mine = mine | (rows == t * nb + j)
    outs = _cross_heads(q_scr[...], mk_ref[...], mv_ref[...])
    for p, o in enumerate(outs):
        sl = slice(p * LANES, (p + 1) * LANES)
        acc_scr[:, sl] = jnp.where(mine, o, acc_scr[:, sl])

    @pl.when(j == nb - 1)
    def _():
        out_ref[...] = x_ref[...] + jnp.dot(acc_scr[...].astype(BF16), wo_ref[...], preferred_element_type=F32)


def _cross_sample(x, g, wq, gq, bd64, cache_k, cache_v, layer, wo, nb, ts):
    T, D = x.shape
    M = cache_k.shape[3]
    mem = pl.BlockSpec((None, None, MIX, M), lambda j: (layer, j, 0, 0))
    return pl.pallas_call(
        functools.partial(_cross_sample_body, nb=nb, ts=ts),
        grid=(nb,),
        in_specs=[_full((T, D)), _full((1, D)), _full(wq.shape), _full((1, MIX)), _full((MIX, MIX)), mem, mem,
                  _full(wo.shape)],
        out_specs=_full((T, D)),
        out_shape=jax.ShapeDtypeStruct((T, D), F32),
        scratch_shapes=[pltpu.VMEM((T, MIX), F32), pltpu.VMEM((T, MIX), F32)],
        compiler_params=_params(("arbitrary",)),
        name="cross_sample",
    )(x, g, wq, gq, bd64, cache_k, cache_v, wo)


def _conv_gate(a0, a1, a2, b, cw):
    ac = cw[3:4] + cw[0:1] * a2 + cw[1:2] * a1 + cw[2:3] * a0
    return ac / (1.0 + jnp.exp(-ac)) * b


def _ffn_prompt_body(x_ref, xh_ref, g_ref, wup_ref, cw_ref, wdn_ref, out_ref, cst_ref, h_scr, a_scr, b_scr,
                     acc_scr, *, tm, F, tf, tiles_per_seq):
    halo = BF16_ROWS
    first = (pl.program_id(0) % tiles_per_seq) == 0
    x = x_ref[...]
    h_scr[0:halo, :] = _rms(xh_ref[...], g_ref[...]).astype(BF16)
    h_scr[halo:halo + tm, :] = _rms(x, g_ref[...]).astype(BF16)
    n = F // tf

    def up(j, slot):
        a = jnp.dot(h_scr[...], wup_ref[:, j * tf:(j + 1) * tf], preferred_element_type=F32)
        a_scr[slot] = a
        a_scr[slot, 0:halo, :] = jnp.where(first, 0.0, a[0:halo])
        b_scr[slot] = jnp.dot(h_scr[halo:halo + tm, :], wup_ref[:, F + j * tf:F + (j + 1) * tf],
                              preferred_element_type=F32)

    up(0, 0)
    for j in range(n):
        slot = j % 2
        cols = slice(j * tf, (j + 1) * tf)
        if j + 1 < n:
            up(j + 1, 1 - slot)
        act = _conv_gate(a_scr[slot, halo:halo + tm, :], a_scr[slot, halo - 1:halo - 1 + tm, :],
                         a_scr[slot, halo - 2:halo - 2 + tm, :], b_scr[slot], cw_ref[:, cols])
        contrib = jnp.dot(act.astype(BF16), wdn_ref[cols, :], preferred_element_type=F32)
        if j == 0:
            acc_scr[...] = contrib
        else:
            acc_scr[...] += contrib
        cst_ref[0, :, cols] = a_scr[slot, tm:tm + halo, :]
    out_ref[...] = x + acc_scr[...]


def _ffn_prompt(x, g, wup, cw, wdn, n_seq):
    T, D = x.shape
    F = wdn.shape[0]
    S = T // n_seq
    tm = min(ROW_TILE, S)
    nt = S // tm
    halo = BF16_ROWS
    tf = FFN_CHUNK if F % FFN_CHUNK == 0 else LANES
    row = pl.BlockSpec((tm, D), lambda i: (i, 0))
    prev = pl.BlockSpec((halo, D), lambda i: (jnp.maximum(i * (tm // halo) - 1, 0), 0))
    return pl.pallas_call(
        functools.partial(_ffn_prompt_body, tm=tm, F=F, tf=tf, tiles_per_seq=nt),
        grid=(T // tm,),
        in_specs=[row, prev, _full((1, D)), _full(wup.shape), _full(cw.shape), _full(wdn.shape)],
        out_specs=[row, pl.BlockSpec((1, halo, F), lambda i: (i // nt, 0, 0))],
        out_shape=[jax.ShapeDtypeStruct((T, D), F32), jax.ShapeDtypeStruct((n_seq, halo, F), F32)],
        scratch_shapes=[pltpu.VMEM((halo + tm, D), BF16), pltpu.VMEM((2, halo + tm, tf), F32),
                        pltpu.VMEM((2, tm, tf), F32), pltpu.VMEM((tm, D), F32)],
        compiler_params=_params(("arbitrary",)),
        name="ffn_prompt",
    )(x, x, g, wup, cw, wdn)


def _ffn_sample_body(x_ref, prev_ref, g_ref, wup_ref, cw_ref, wdn_ref, out_ref, cst_ref, *, nb, F, tf):
    x = x_ref[...]
    T = x.shape[0]
    h = _rms(x, g_ref[...]).astype(BF16)
    acc = None
    for j in range(F // tf):
        cols = slice(j * tf, (j + 1) * tf)
        a = jnp.dot(h, wup_ref[:, cols], preferred_element_type=F32)
        b = jnp.dot(h, wup_ref[:, F + j * tf:F + (j + 1) * tf], preferred_element_type=F32)
        prev = prev_ref[:, cols]
        a1 = jnp.concatenate([prev[nb:2 * nb], a[:T - nb]], axis=0)
        a2 = jnp.concatenate([prev, a[:T - 2 * nb]], axis=0)
        act = _conv_gate(a, a1, a2, b, cw_ref[:, cols])
        contrib = jnp.dot(act.astype(BF16), wdn_ref[cols, :], preferred_element_type=F32)
        acc = contrib if acc is None else acc + contrib
        cst_ref[:, cols] = a[T - 2 * nb:]
    out_ref[...] = x + acc


def _ffn_sample(x, prev, g, wup, cw, wdn, nb):
    T, D = x.shape
    F = wdn.shape[0]
    tf = FFN_CHUNK if F % FFN_CHUNK == 0 else LANES
    return pl.pallas_call(
        functools.partial(_ffn_sample_body, nb=nb, F=F, tf=tf),
        grid=(1,),
        in_specs=[_full((T, D)), _full(prev.shape), _full((1, D)), _full(wup.shape), _full(cw.shape),
                  _full(wdn.shape)],
        out_specs=[_full((T, D)), _full(prev.shape)],
        out_shape=[jax.ShapeDtypeStruct((T, D), F32), jax.ShapeDtypeStruct(prev.shape, F32)],
        compiler_params=_params(("arbitrary",)),
        name="ffn_sample",
    )(x, prev, g, wup, cw, wdn)


def _local_sample_body(xb_ref, st_ref, uc_ref, vn_ref, wp_ref, sp_ref, coef_ref, bias_ref, ob_ref, oc_ref,
                       *, nb, ts, n_past):
    wl = _lane_window()
    ext = jnp.concatenate([st_ref[...], xb_ref[...]], axis=0)
    base = POOL_STATE * nb
    x = xb_ref[...]
    acc = x
    for k in range(1, POOL_STATE + 1):
        acc = acc + jnp.where(k < wl, ext[base - k * nb:base - k * nb + ts * nb], 0.0)
    for t in range(ts):
        rows = slice(t * nb, (t + 1) * nb)
        cnt = jnp.minimum(n_past + t + 1, wl).astype(F32)
        d = acc[rows] / cnt - x[rows]
        y = jnp.dot(d.astype(BF16), wp_ref[...], preferred_element_type=F32)
        ob_ref[rows, :] = y * sp_ref[...]
        mixed = bias_ref[t:t + 1, :]
        for s in range(t + 1):
            mixed = mixed + coef_ref[t * ts + s:t * ts + s + 1, :] * vn_ref[s * nb:(s + 1) * nb, :]
        oc_ref[rows, :] = uc_ref[rows, :] * mixed


def _local_sample(xb, state, uc, vn, wp_bd, sp, coef, bias, nb, ts, n_past):
    T = xb.shape[0]
    return pl.pallas_call(
        functools.partial(_local_sample_body, nb=nb, ts=ts, n_past=n_past),
        grid=(1,),
        in_specs=[_full((T, MIX)), _full(state.shape), _full((T, MIX)), _full((T, MIX)), _full((MIX, MIX)),
                  _full((1, MIX)), _full(coef.shape), _full(bias.shape)],
        out_specs=[_full((T, MIX))] * 2,
        out_shape=[jax.ShapeDtypeStruct((T, MIX), F32)] * 2,
        compiler_params=_params(("arbitrary",)),
        name="local_sample",
    )(xb, state, uc, vn, wp_bd, sp, coef, bias)


Q_PAD = 8


def _slot_queries(q8, slot_width):
    lane = lax.broadcasted_iota(jnp.int32, (1, MIX), 1)
    slots = [jnp.where((lane >= i * slot_width) & (lane < (i + 1) * slot_width), q8, 0.0)
             for i in range(MIX // slot_width)]
    return jnp.concatenate(slots, axis=0).astype(BF16)


def _paged_fetch(pt_ref, cache_k, cache_v, kbuf, vbuf, sem, *, layer, G, reverse):
    b, j = pl.program_id(0), pl.program_id(1)
    nb, nj = pl.num_programs(0), pl.num_programs(1)
    step = b * nj + j
    slot = step % 2

    def copies(bb, jj, sl):
        group = nj - 1 - jj if reverse else jj
        out = []
        for g in range(G):
            pid = pt_ref[bb, group * G + g]
            out.append(pltpu.make_async_copy(cache_k.at[layer, pid], kbuf.at[sl, g], sem.at[sl, 0, g]))
            out.append(pltpu.make_async_copy(cache_v.at[layer, pid], vbuf.at[sl, g], sem.at[sl, 1, g]))
        return out

    @pl.when(step == 0)
    def _():
        for c in copies(b, j, slot):
            c.start()

    @pl.when(step + 1 < nb * nj)
    def _():
        wraps = j == nj - 1
        for c in copies(jnp.where(wraps, b + 1, b), jnp.where(wraps, 0, j + 1), 1 - slot):
            c.start()

    for c in copies(b, j, slot):
        c.wait()
    return [kbuf.at[slot, g] for g in range(G)], [vbuf.at[slot, g] for g in range(G)]


def _segment(page_refs, s, pages_per_seg):
    parts = [page_refs[s * pages_per_seg + g][...].astype(BF16) for g in range(pages_per_seg)]
    return parts[0] if pages_per_seg == 1 else jnp.concatenate(parts, axis=1)


def _sb_decode_body(pt_ref, q_ref, kn_ref, vn_ref, ck_ref, cv_ref, o_ref, acc_ref, car_ref, kbuf, vbuf, sem,
                    *, G, scale, layer):
    k_refs, v_refs = _paged_fetch(pt_ref, ck_ref, cv_ref, kbuf, vbuf, sem, layer=layer, G=G, reverse=True)
    j = pl.program_id(1)
    rows = (MIX // HEAD) * Q_PAD
    qbd = _slot_queries(q_ref[0] * (scale * LOG2_E), HEAD)
    page = k_refs[0].shape[1]
    seg = max(ATT_BLOCK, page)
    r = lax.broadcasted_iota(jnp.int32, (seg, seg), 0)
    c = lax.broadcasted_iota(jnp.int32, (seg, seg), 1)
    upper = (r > c).astype(BF16)

    def chain(kb, vb, up, valid):
        z2 = jnp.dot(qbd, kb, preferred_element_type=F32)
        ls, lk = _log2_sigmoids(z2)
        if valid is not None:
            lk = jnp.where(valid, lk, 0.0)
        bt = jnp.dot(lk.astype(BF16), up, preferred_element_type=F32)
        w = jnp.exp2(ls + bt)
        if valid is not None:
            w = jnp.where(valid, w, 0.0)
        pv = lax.dot_general(w.astype(BF16), vb, _NT, preferred_element_type=F32)
        return pv, bt[:, 0:1] + lk[:, 0:1]

    def fold(parts):
        acc, car = acc_ref[...], car_ref[...]
        for pv, tot in parts:
            acc = acc + jnp.exp2(car) * pv
            car = car + tot
        acc_ref[...] = acc
        car_ref[...] = car

    @pl.when(j == 0)
    def _():
        acc_ref[...] = jnp.zeros(acc_ref.shape, F32)
        car_ref[...] = jnp.zeros(car_ref.shape, F32)
        n = kn_ref.shape[2]
        t = lax.broadcasted_iota(jnp.int32, (rows, n), 0) & (Q_PAD - 1)
        s = lax.broadcasted_iota(jnp.int32, (rows, n), 1)
        fold([chain(kn_ref[0].astype(BF16), vn_ref[0].astype(BF16), upper[:n, :n], s < t)])

    pps = seg // page
    fold([chain(_segment(k_refs, s, pps), _segment(v_refs, s, pps), upper, None)
          for s in range(G // pps - 1, -1, -1)])

    @pl.when(j == pl.num_programs(1) - 1)
    def _():
        lane = lax.broadcasted_iota(jnp.int32, (1, MIX), 1)
        o = jnp.zeros((Q_PAD, MIX), F32)
        for h in range(MIX // HEAD):
            in_head = (lane >= h * HEAD) & (lane < (h + 1) * HEAD)
            o = jnp.where(in_head, acc_ref[h * Q_PAD:(h + 1) * Q_PAD, :], o)
        o_ref[0] = o


def _df_decode_body(pt_ref, lam_ref, gs_ref, q_ref, kn_ref, vn_ref, ck_ref, cv_ref, o_ref, acc_ref, m_ref, l_ref,
                    kbuf, vbuf, sem, *, G, scale, lam_init, layer):
    k_refs, v_refs = _paged_fetch(pt_ref, ck_ref, cv_ref, kbuf, vbuf, sem, layer=layer, G=G, reverse=False)
    j = pl.program_id(1)
    slots = MIX // DF_QK
    rows = slots * Q_PAD
    qbd = _slot_queries(q_ref[0] * (scale * LOG2_E), DF_QK)
    page = k_refs[0].shape[1]
    seg = max(ATT_BLOCK, page)

    def block(kbs, vbs, valid):
        ss = [jnp.dot(qbd, kb, preferred_element_type=F32) for kb in kbs]
        if valid is not None:
            ss = [jnp.where(valid, s, NEG_INF) for s in ss]
        top = ss[0]
        for s in ss[1:]:
            top = jnp.maximum(top, s)
        m_old = m_ref[...]
        m_new = jnp.maximum(m_old, jnp.max(top, axis=-1, keepdims=True))
        a = jnp.exp2(m_old - m_new)
        ps = [jnp.exp2(s - m_new) for s in ss]
        psum = ps[0]
        for p in ps[1:]:
            psum = psum + p
        pv = None
        for p, vb in zip(ps, vbs):
            d = lax.dot_general(p.astype(BF16), vb, _NT, preferred_element_type=F32)
            pv = d if pv is None else pv + d
        l_ref[...] = a * l_ref[...] + jnp.sum(psum, axis=-1, keepdims=True)
        acc_ref[...] = a * acc_ref[...] + pv
        m_ref[...] = m_new

    @pl.when(j == 0)
    def _():
        acc_ref[...] = jnp.zeros(acc_ref.shape, F32)
        l_ref[...] = jnp.zeros(l_ref.shape, F32)
        m_ref[...] = jnp.full(m_ref.shape, NEG_INF, F32)
        n = kn_ref.shape[2]
        t = lax.broadcasted_iota(jnp.int32, (rows, n), 0) & (Q_PAD - 1)
        s = lax.broadcasted_iota(jnp.int32, (rows, n), 1)
        block([kn_ref[0].astype(BF16)], [vn_ref[0].astype(BF16)], s <= t)

    pps = seg // page
    nseg = G // pps
    block([_segment(k_refs, s, pps) for s in range(nseg)], [_segment(v_refs, s, pps) for s in range(nseg)], None)

    @pl.when(j == pl.num_programs(1) - 1)
    def _():
        lane = lax.broadcasted_iota(jnp.int32, (1, MIX), 1)
        lam = _diff_lambda(lam_ref, lam_init)
        norm = acc_ref[...] / l_ref[...]
        o = jnp.zeros((Q_PAD, MIX), F32)
        for h in range(MIX // HEAD):
            in_head = (lane >= h * HEAD) & (lane < (h + 1) * HEAD)
            r0 = (2 * h) * Q_PAD
            oh = norm[r0:r0 + Q_PAD, :] - lam * norm[r0 + Q_PAD:r0 + 2 * Q_PAD, :]
            o = jnp.where(in_head, oh, o)
        lane2 = lax.broadcasted_iota(jnp.int32, (1, LANES), 1)
        parts = [_head_rms(o[:, p * LANES:(p + 1) * LANES], gs_ref[...], lane2 < HEAD, 1.0 - lam_init)
                 for p in range(MIX // LANES)]
        o_ref[0] = jnp.concatenate(parts, axis=-1)


def _decode_specs(cache_k, n_pages, G):
    nj = n_pages // G
    page = cache_k.shape[3]
    tok = pl.BlockSpec((1, Q_PAD, MIX), lambda b, j, pt: (b, 0, 0))
    new = pl.BlockSpec((1, MIX, LANES), lambda b, j, pt: (b, 0, 0))
    hbm = pl.BlockSpec(memory_space=pl.ANY)
    fetch_scratch = [pltpu.VMEM((2, G, MIX, page), F32), pltpu.VMEM((2, G, MIX, page), F32),
                     pltpu.SemaphoreType.DMA((2, 2, G))]
    return nj, tok, new, hbm, fetch_scratch


def _sb_decode(page_table, q, k_new, v_new, cache_k, cache_v, layer):
    nb, n_pages = page_table.shape
    G = PAGES_PER_STEP
    nj, tok, new, hbm, fetch_scratch = _decode_specs(cache_k, n_pages, G)
    rows = (MIX // HEAD) * Q_PAD
    grid_spec = pltpu.PrefetchScalarGridSpec(
        num_scalar_prefetch=1,
        grid=(nb, nj),
        in_specs=[tok, new, new, hbm, hbm],
        out_specs=tok,
        scratch_shapes=[pltpu.VMEM((rows, MIX), F32), pltpu.VMEM((rows, 1), F32)] + fetch_scratch,
    )
    return pl.pallas_call(
        functools.partial(_sb_decode_body, G=G, scale=HEAD ** -0.5, layer=layer),
        grid_spec=grid_spec,
        out_shape=jax.ShapeDtypeStruct((nb, Q_PAD, MIX), F32),
        compiler_params=_params(("arbitrary", "arbitrary")),
        name="sb_decode",
    )(page_table, q, k_new, v_new, cache_k, cache_v)


def _df_decode(page_table, lam_rows, gs2, q, k_new, v_new, cache_k, cache_v, layer, lam_init):
    nb, n_pages = page_table.shape
    G = PAGES_PER_STEP
    nj, tok, new, hbm, fetch_scratch = _decode_specs(cache_k, n_pages, G)
    rows = (MIX // DF_QK) * Q_PAD
    const = lambda shape: pl.BlockSpec(shape, lambda b, j, pt: (0,) * len(shape))
    grid_spec = pltpu.PrefetchScalarGridSpec(
        num_scalar_prefetch=1,
        grid=(nb, nj),
        in_specs=[const(lam_rows.shape), const((1, LANES)), tok, new, new, hbm, hbm],
        out_specs=tok,
        scratch_shapes=[pltpu.VMEM((rows, MIX), F32), pltpu.VMEM((rows, 1), F32), pltpu.VMEM((rows, 1), F32)]
        + fetch_scratch,
    )
    return pl.pallas_call(
        functools.partial(_df_decode_body, G=G, scale=DF_QK ** -0.5, lam_init=lam_init, layer=layer),
        grid_spec=grid_spec,
        out_shape=jax.ShapeDtypeStruct((nb, Q_PAD, MIX), F32),
        compiler_params=_params(("arbitrary", "arbitrary")),
        name="df_decode",
    )(page_table, lam_rows, gs2, q, k_new, v_new, cache_k, cache_v)


def _row(v):
    return v.reshape(1, -1).astype(F32)


def _tile_row(v, width):
    return jnp.tile(v.astype(F32), width // v.shape[0]).reshape(1, width)


def _pad_rows(a, rows):
    return jnp.pad(a, ((0, rows - a.shape[0]),) + ((0, 0),) * (a.ndim - 1))


def _block_diag(w):
    G, c, d = w.shape
    eye = jnp.eye(G, dtype=w.dtype)
    return (eye[:, None, :, None] * w[:, :, None, :]).reshape(G * c, G * d)


def kernel(x_prompt, x_sample, mem_prompt, cache_sb_k, cache_sb_v, cache_df_k, cache_df_v, cache_mem_k, cache_mem_v, state_pool, state_ffn_conv, page_table, g_mix, w_in, w_branch, w_out, w_pool, s_pool, g_gm, w_spatial, b_spatial, g_qd, g_kd, lam_q1, lam_k1, lam_q2, lam_k2, g_subln, g_cross, w_cq, w_ckv, w_co, g_cq, g_ck, g_ffn, w_up, w_conv, b_conv, w_down):
    B, S, D = x_prompt.shape
    NB, TS, _ = x_sample.shape
    depth = w_in.shape[0]
    M = mem_prompt.shape[1]
    F = w_down.shape[1]
    page = cache_sb_k.shape[2]
    n_pages = page_table.shape[1]
    n_past = n_pages * page
    heads = MIX // HEAD
    assert TS >= CONV_W - 1 and TS <= Q_PAD and TS <= GM_CHUNK and n_pages % PAGES_PER_STEP == 0

    bd32 = _block_diag_mean(MIX, DF_QK)
    bd64 = _block_diag_mean(MIX, HEAD)
    feat_major = lambda c: c.transpose(0, 1, 3, 4, 2).reshape(c.shape[0], c.shape[1], MIX, c.shape[2])
    csb_k, csb_v, cdf_k, cdf_v = (feat_major(c) for c in (cache_sb_k, cache_sb_v, cache_df_k, cache_df_v))
    cmem_k, cmem_v = feat_major(cache_mem_k), feat_major(cache_mem_v)
    tok_major = lambda a: a.reshape(a.shape[0], heads, HEAD, a.shape[2]).transpose(0, 3, 1, 2)

    xp = x_prompt.reshape(B * S, D)
    xs = x_sample.transpose(1, 0, 2).reshape(TS * NB, D)
    mem = mem_prompt.reshape(B * M, D)

    to_batch = lambda a: a.reshape(TS, NB, MIX).transpose(1, 0, 2)
    to_time = lambda a: a[:, :TS].transpose(1, 0, 2).reshape(TS * NB, MIX)
    pad_tok = lambda a, n: jnp.pad(to_batch(a), ((0, 0), (0, n - TS), (0, 0)))
    new_page = lambda a: jnp.pad(to_batch(a).transpose(0, 2, 1), ((0, 0), (0, 0), (0, LANES - TS)))
    kv_cols = lambda w: jnp.concatenate([w[:, i * MIX:(i + 1) * MIX] for i in (1, 2, 7, 8)], axis=1)
    tok_cols = lambda w: jnp.concatenate([w[:, i * MIX:(i + 1) * MIX] for i in (0, 3, 4, 5, 6)], axis=1)

    outs = [[] for _ in range(15)]
    for l in range(depth):
        lam_init = 0.8 - 0.6 * math.exp(-0.3 * l)
        w_l = w_in[l]
        wtok = tok_cols(w_l).astype(BF16)
        wkv = kv_cols(w_l).astype(BF16)
        wkv_t = wkv.T
        wg = w_l[:, 9 * MIX:].astype(BF16)
        wb = w_branch[l].astype(BF16)
        wo = w_out[l].astype(BF16)
        gq, gk = _tile_row(g_qd[l], MIX), _tile_row(g_kd[l], MIX)
        gk_col = gk.reshape(MIX, 1)
        ggm, gmix = _row(g_gm[l]), _row(g_mix[l])
        wp_bd = _block_diag(w_pool[l]).astype(BF16)
        sp = _row(s_pool[l])
        bs_full = jnp.repeat(b_spatial[l].T, MIX // GM_GROUPS, axis=1).astype(F32)
        lam_rows = _pad_rows(jnp.stack([lam_q1[l], lam_k1[l], lam_q2[l], lam_k2[l]]).astype(F32), 8)
        gs2 = _tile_row(g_subln[l], LANES)
        wcq = w_cq[l].astype(BF16)
        wckv_t = w_ckv[l].astype(BF16).T
        wco = w_co[l].astype(BF16)
        gcq, gck_col = _tile_row(g_cq[l], MIX), _tile_row(g_ck[l], MIX).reshape(MIX, 1)
        wup = w_up[l].astype(BF16)
        wdn = w_down[l].astype(BF16)
        cw = _pad_rows(jnp.concatenate([w_conv[l], b_conv[l][None]], axis=0).astype(F32), 8)

        qa, xb, uc, vn, qd, ka, va, kd, vd = _proj_in(xp, gmix, wtok, wkv_t, gq, gk_col, ggm, bd32, B, True)
        oa = _sb_prompt(qa, ka, va, B)
        ob, oc = _local_prompt(xb, uc, vn, wp_bd, sp, w_spatial[l].astype(F32), bs_full, B)
        raw_exp_ok = (DF_QK ** 0.5 * 1.01 * jnp.max(jnp.abs(g_qd[l])) * jnp.max(jnp.abs(g_kd[l]))
                      <= DF_RAW_EXP_BOUND)
        od = lax.cond(raw_exp_ok,
                      functools.partial(_df_prompt, n_seq=B, lam_init=lam_init, shifted=False),
                      functools.partial(_df_prompt, n_seq=B, lam_init=lam_init, shifted=True),
                      lam_rows, gs2, qd, kd, vd)
        xp = _merge(xp, gmix, oa, ob, oc, od, wg, wb, wo)
        mk, mv = _memkv(mem, wckv_t, gck_col, bd64, B)
        xp = _cross_prompt(xp, _row(g_cross[l]), wcq, gcq, bd64, mk, mv, wco, B)
        xp, cst = _ffn_prompt(xp, _row(g_ffn[l]), wup, cw, wdn, B)
        outs[0].append(tok_major(ka))
        outs[1].append(tok_major(va))
        outs[2].append(tok_major(kd))
        outs[3].append(tok_major(vd))
        outs[4].append(tok_major(mk))
        outs[5].append(tok_major(mv))
        outs[6].append(xb.reshape(B, S, MIX)[:, S - POOL_STATE:])
        outs[7].append(cst[:, -(CONV_W - 1):])

        qa, xb, uc, vn, qd, ka, va, kd, vd = _proj_in(xs, gmix, wtok, wkv, gq, gk, ggm, bd32, 1, False)
        oa = _sb_decode(page_table, pad_tok(qa, Q_PAD), new_page(ka), new_page(va), csb_k, csb_v, l)
        od = _df_decode(page_table, lam_rows, gs2, pad_tok(qd, Q_PAD), new_page(kd), new_page(vd),
                        cdf_k, cdf_v, l, lam_init)
        state_t = state_pool[l].astype(F32).transpose(1, 0, 2).reshape(POOL_STATE * NB, MIX)
        ws_t = jnp.where(jnp.tril(jnp.ones((TS, TS), bool)), w_spatial[l][:, :TS, :TS], 0.0)
        coef = _pad_rows(jnp.repeat(ws_t.transpose(1, 2, 0).reshape(TS * TS, GM_GROUPS),
                                    MIX // GM_GROUPS, axis=1).astype(F32), -(-TS * TS // 8) * 8)
        bias = _pad_rows(bs_full[:TS], 8)
        ob, oc = _local_sample(xb, state_t, uc, vn, wp_bd, sp, coef, bias, NB, TS, n_past)
        xs = _merge(xs, gmix, to_time(oa), ob, oc, to_time(od), wg, wb, wo)
        xs = _cross_sample(xs, _row(g_cross[l]), wcq, gcq, bd64, cmem_k, cmem_v, l, wco, NB, TS)
        prev_t = state_ffn_conv[l].astype(F32).transpose(1, 0, 2).reshape((CONV_W - 1) * NB, F)
        xs, cst_s = _ffn_sample(xs, prev_t, _row(g_ffn[l]), wup, cw, wdn, NB)
        heads_s = lambda a: to_batch(a).reshape(NB, TS, heads, HEAD)
        outs[8].append(heads_s(ka))
        outs[9].append(heads_s(va))
        outs[10].append(heads_s(kd))
        outs[11].append(heads_s(vd))
        pool_ext = jnp.concatenate([state_pool[l].astype(F32), to_batch(xb)], axis=1)
        outs[12].append(pool_ext[:, -POOL_STATE:])
        outs[13].append(cst_s.reshape(CONV_W - 1, NB, F).transpose(1, 0, 2))
        outs[14].append(to_batch(vn))

    y_prompt = xp.reshape(B, S, D)
    y_sample = xs.reshape(TS, NB, D).transpose(1, 0, 2)
    return (y_prompt, y_sample) + tuple(jnp.stack(o) for o in outs)
```

```python
import functools
import math

import numpy as np
import jax
import jax.numpy as jnp
from jax import lax
from jax.experimental import pallas as pl
from jax.experimental.pallas import tpu as pltpu

F32 = jnp.float32
BF16 = jnp.bfloat16

EPS = 1e-6
NEG_INF = -1e30
LOG2_E = math.log2(math.e)
MIX = 256
HEAD = 64
DF_QK = 32
POOL_WINDOWS = (2, 4, 8, 16)
POOL_STATE = max(POOL_WINDOWS) - 1
POOL_PAD = 16
GM_CHUNK = 128
GM_GROUPS = 4
CONV_W = 3
LANES = 128
BF16_ROWS = 16
ROW_TILE = 512
ATT_BLOCK = 256
FFN_CHUNK = 256
PAGES_PER_STEP = 16
DECODE_SLOTS = 3
DF_RAW_EXP_BOUND = 40.0
VMEM_LIMIT = 56 * 1024 * 1024

_NT = (((1,), (1,)), ((), ()))


def _params(sem):
    return pltpu.CompilerParams(dimension_semantics=sem, vmem_limit_bytes=VMEM_LIMIT)


def _rms(x, g):
    return x * lax.rsqrt(jnp.mean(x * x, axis=-1, keepdims=True) + EPS) * g


def _group_rms(v, g, bd):
    ms = jnp.dot((v * v).astype(BF16), bd, preferred_element_type=F32)
    return v * lax.rsqrt(ms + EPS) * g


def _block_diag_mean(width, group):
    i = np.arange(width)
    return jnp.asarray((i[:, None] // group == i[None, :] // group) / group, BF16)


def _full(shape):
    return pl.BlockSpec(shape, lambda *_: (0,) * len(shape))


def _lane_window():
    lane = lax.broadcasted_iota(jnp.int32, (1, MIX), 1)
    gdim = MIX // len(POOL_WINDOWS)
    wl = jnp.full((1, MIX), POOL_WINDOWS[-1], jnp.int32)
    for g in range(len(POOL_WINDOWS) - 2, -1, -1):
        wl = jnp.where(lane < (g + 1) * gdim, POOL_WINDOWS[g], wl)
    return wl


def _proj_in_body(x_ref, g_ref, wtok_ref, wkv_ref, gq_ref, gk_ref, ggm_ref, bd_ref,
                  qa_ref, xb_ref, uc_ref, vn_ref, qd_ref, ka_ref, va_ref, kd_ref, vd_ref, *, kv_major):
    h = _rms(x_ref[...], g_ref[...]).astype(BF16)
    y = jnp.dot(h, wtok_ref[...], preferred_element_type=F32)
    part = lambda i: y[:, i * MIX:(i + 1) * MIX]
    qa_ref[...] = part(0)
    xb_ref[...] = part(1)
    uc_ref[...] = part(2)
    vn_ref[...] = _rms(part(3), ggm_ref[...])
    qd_ref[...] = _group_rms(part(4), gq_ref[...], bd_ref[...])
    if kv_major:
        yt = lax.dot_general(wkv_ref[...], h, _NT, preferred_element_type=F32)
        kd = yt[2 * MIX:3 * MIX]
        ms = jnp.dot(bd_ref[...], (kd * kd).astype(BF16), preferred_element_type=F32)
        ka_ref[...] = yt[0:MIX]
        va_ref[...] = yt[MIX:2 * MIX]
        kd_ref[...] = kd * lax.rsqrt(ms + EPS) * gk_ref[...]
        vd_ref[...] = yt[3 * MIX:4 * MIX]
    else:
        y2 = jnp.dot(h, wkv_ref[...], preferred_element_type=F32)
        ka_ref[...] = y2[:, 0:MIX]
        va_ref[...] = y2[:, MIX:2 * MIX]
        kd_ref[...] = _group_rms(y2[:, 2 * MIX:3 * MIX], gk_ref[...], bd_ref[...])
        vd_ref[...] = y2[:, 3 * MIX:4 * MIX]


def _proj_in(x, g, wtok, wkv, gq, gk, ggm, bd32, n_seq, kv_major):
    T, D = x.shape
    S = T // n_seq
    tm = min(ROW_TILE, S)
    nt = S // tm
    row = pl.BlockSpec((tm, D), lambda i: (i, 0))
    out = pl.BlockSpec((tm, MIX), lambda i: (i, 0))
    tok_shape = jax.ShapeDtypeStruct((T, MIX), F32)
    if kv_major:
        kv_spec = pl.BlockSpec((None, MIX, tm), lambda i: (i // nt, 0, i % nt))
        kv_shape = jax.ShapeDtypeStruct((n_seq, MIX, S), F32)
    else:
        kv_spec, kv_shape = out, tok_shape
    return pl.pallas_call(
        functools.partial(_proj_in_body, kv_major=kv_major),
        grid=(T // tm,),
        in_specs=[row, _full((1, D)), _full(wtok.shape), _full(wkv.shape), _full((1, MIX)), _full(gk.shape),
                  _full((1, MIX)), _full((MIX, MIX))],
        out_specs=[out] * 5 + [kv_spec] * 4,
        out_shape=[tok_shape] * 5 + [kv_shape] * 4,
        compiler_params=_params(("parallel",)),
        name="proj_in",
    )(x, g, wtok, wkv, gq, gk, ggm, bd32)


def _local_prompt_body(xb_ref, uc_ref, vn_ref, wp_ref, sp_ref, ws_ref, bs_ref, ob_ref, oc_ref, ext_ref):
    S = xb_ref.shape[0]
    ext_ref[0:POOL_PAD, :] = jnp.zeros((POOL_PAD, MIX), F32)
    ext_ref[POOL_PAD:POOL_PAD + S, :] = xb_ref[...]
    wl = _lane_window()
    lane = lax.broadcasted_iota(jnp.int32, (1, MIX), 1)
    R = min(256, S)
    for c in range(S // R):
        x = xb_ref[c * R:(c + 1) * R, :]
        acc = x
        for k in range(1, POOL_STATE + 1):
            lo = POOL_PAD + c * R - k
            acc = acc + jnp.where(k < wl, ext_ref[lo:lo + R, :], 0.0)
        pos = c * R + lax.broadcasted_iota(jnp.int32, (R, 1), 0)
        cnt = jnp.minimum(pos + 1, wl).astype(F32)
        d = acc / cnt - x
        y = jnp.dot(d.astype(BF16), wp_ref[...], preferred_element_type=F32)
        ob_ref[c * R:(c + 1) * R, :] = y * sp_ref[...]

    C = min(GM_CHUNK, S)
    r = lax.broadcasted_iota(jnp.int32, (C, C), 0)
    cidx = lax.broadcasted_iota(jnp.int32, (C, C), 1)
    gdim = MIX // GM_GROUPS
    ws = [jnp.where(r >= cidx, ws_ref[g][:C, :C], 0.0).astype(BF16) for g in range(GM_GROUPS)]
    bias = bs_ref[0:C, :]
    for n in range(S // C):
        vn = vn_ref[n * C:(n + 1) * C, :].astype(BF16)
        mixed = jnp.dot(ws[0], vn, preferred_element_type=F32)
        for g in range(1, GM_GROUPS):
            mixed = jnp.where(lane >= g * gdim, jnp.dot(ws[g], vn, preferred_element_type=F32), mixed)
        oc_ref[n * C:(n + 1) * C, :] = uc_ref[n * C:(n + 1) * C, :] * (mixed + bias)


def _local_prompt(xb, uc, vn, wp_bd, sp, ws, bs_full, n_seq):
    T = xb.shape[0]
    S = T // n_seq
    seq = pl.BlockSpec((S, MIX), lambda b: (b, 0))
    return pl.pallas_call(
        _local_prompt_body,
        grid=(n_seq,),
        in_specs=[seq, seq, seq, _full((MIX, MIX)), _full((1, MIX)), _full(ws.shape), _full(bs_full.shape)],
        out_specs=[seq, seq],
        out_shape=[jax.ShapeDtypeStruct((T, MIX), F32)] * 2,
        scratch_shapes=[pltpu.VMEM((POOL_PAD + S, MIX), F32)],
        compiler_params=_params(("parallel",)),
        name="local_prompt",
    )(xb, uc, vn, wp_bd, sp, ws, bs_full)


def _log_sigmoids(z):
    ls = jnp.minimum(z, 0.0) - jnp.log(1.0 + jnp.exp(-jnp.abs(z)))
    return ls, ls - z


def _suffix_sums(lk, upper):
    hi = lk.astype(BF16)
    lo = (lk - hi.astype(F32)).astype(BF16)
    return (jnp.dot(hi, upper, preferred_element_type=F32)
            + jnp.dot(lo, upper, preferred_element_type=F32))


def _log2_sigmoids(z2):
    ls2 = jnp.minimum(z2, 0.0) - jnp.log2(1.0 + jnp.exp2(-jnp.abs(z2)))
    return ls2, ls2 - z2


def _sweep_blocks(qi, scores, tile, bufs):
    a, b = bufs
    scores(qi, a)
    tile(qi, a, b, True)

    def pair(p, carry):
        j = qi - 1 - 2 * p
        tile(j, b, a, False)
        tile(j - 1, a, b, False)
        return carry

    lax.fori_loop(0, qi // 2, pair, 0)

    @pl.when(qi % 2 == 1)
    def _():
        tile(0, b, a, False)


def _sb_prompt_body(q_ref, k_ref, v_ref, o_ref, acc_ref, car_ref, za_ref, zb_ref, *, blk, scale):
    qi = pl.program_id(2)
    lane = lax.broadcasted_iota(jnp.int32, (1, LANES), 1)
    q = q_ref[...] * (scale * LOG2_E)
    qh = [jnp.where(lane < HEAD, q, 0.0).astype(BF16), jnp.where(lane >= HEAD, q, 0.0).astype(BF16)]
    r = lax.broadcasted_iota(jnp.int32, (blk, blk), 0)
    c = lax.broadcasted_iota(jnp.int32, (blk, blk), 1)
    upper = (r > c).astype(BF16)
    causal = c < r

    heads = range(2)

    def scores(j, buf):
        start = pl.multiple_of(j * blk, blk)
        k = k_ref[:, pl.ds(start, blk)].astype(BF16)
        for h in heads:
            buf[h] = jnp.dot(qh[h], k, preferred_element_type=F32)

    def tile(j, cur, nxt, diag):
        scores(jnp.maximum(j - 1, 0), nxt)
        start = pl.multiple_of(j * blk, blk)
        v = v_ref[:, pl.ds(start, blk)].astype(BF16)
        zs = [cur[h] for h in heads]
        lss, lks = zip(*[_log2_sigmoids(z) for z in zs])
        if diag:
            lks = [jnp.where(causal, lk, 0.0) for lk in lks]
        bts = [jnp.dot(lks[h].astype(BF16), upper, preferred_element_type=F32) for h in heads]
        ws = [jnp.exp2(lss[h] + bts[h]) for h in heads]
        if diag:
            ws = [jnp.where(causal, w, 0.0) for w in ws]
        pvs = [lax.dot_general(ws[h].astype(BF16), v, _NT, preferred_element_type=F32) for h in heads]
        for h in heads:
            tot = bts[h][:, 0:1] + lks[h][:, 0:1]
            if diag:
                acc_ref[h] = pvs[h]
                car_ref[h] = tot
            else:
                acc_ref[h] += jnp.exp2(car_ref[h]) * pvs[h]
                car_ref[h] += tot

    _sweep_blocks(qi, scores, tile, (za_ref, zb_ref))
    o_ref[...] = jnp.where(lane < HEAD, acc_ref[0], acc_ref[1])


def _sb_prompt(q, k, v, n_seq):
    T = q.shape[0]
    S = T // n_seq
    blk = min(ATT_BLOCK, S)
    nq = S // blk
    pairs = MIX // LANES
    qspec = pl.BlockSpec((blk, LANES), lambda b, p, i: (b * nq + i, p))
    kspec = pl.BlockSpec((None, LANES, S), lambda b, p, i: (b, p, 0))
    return pl.pallas_call(
        functools.partial(_sb_prompt_body, blk=blk, scale=HEAD ** -0.5),
        grid=(n_seq, pairs, nq),
        in_specs=[qspec, kspec, kspec],
        out_specs=qspec,
        out_shape=jax.ShapeDtypeStruct((T, MIX), F32),
        scratch_shapes=[pltpu.VMEM((2, blk, LANES), F32), pltpu.VMEM((2, blk, 1), F32),
                        pltpu.VMEM((2, blk, blk), F32), pltpu.VMEM((2, blk, blk), F32)],
        compiler_params=_params(("parallel", "parallel", "arbitrary")),
        name="sb_prompt",
    )(q, k, v)


def _diff_lambda(lam_ref, lam_init):
    s1 = jnp.sum(lam_ref[0:1, :] * lam_ref[1:2, :], axis=-1, keepdims=True)
    s2 = jnp.sum(lam_ref[2:3, :] * lam_ref[3:4, :], axis=-1, keepdims=True)
    return jnp.exp(s1) - jnp.exp(s2) + lam_init


def _head_rms(o, g, lane_in_head0, out_scale):
    o2 = o * o
    s0 = jnp.sum(jnp.where(lane_in_head0, o2, 0.0), axis=-1, keepdims=True)
    s1 = jnp.sum(jnp.where(lane_in_head0, 0.0, o2), axis=-1, keepdims=True)
    ms = jnp.where(lane_in_head0, s0, s1) * (1.0 / HEAD)
    return o * lax.rsqrt(ms + EPS) * g * out_scale


def _df_prompt_body(lam_ref, gs_ref, q_ref, k_ref, v_ref, o_ref, acc_ref, *stats, blk, scale, lam_init, shifted):
    qi = pl.program_id(2)
    lane = lax.broadcasted_iota(jnp.int32, (1, LANES), 1)
    q = q_ref[...] * (scale * LOG2_E)
    qm = [jnp.where((lane >= i * DF_QK) & (lane < (i + 1) * DF_QK), q, 0.0).astype(BF16) for i in range(4)]
    r = lax.broadcasted_iota(jnp.int32, (blk, blk), 0)
    c = lax.broadcasted_iota(jnp.int32, (blk, blk), 1)
    causal = c <= r
    if shifted:
        m_ref, l_ref = stats
        bufs = (None, None)
    else:
        bufs = stats

    def scores(j, buf):
        if shifted:
            return
        start = pl.multiple_of(j * blk, blk)
        k = k_ref[:, pl.ds(start, blk)].astype(BF16)
        for i in range(4):
            buf[i] = jnp.dot(qm[i], k, preferred_element_type=F32)

    def tile_raw(j, cur, nxt, diag):
        scores(jnp.maximum(j - 1, 0), nxt)
        start = pl.multiple_of(j * blk, blk)
        v = v_ref[:, pl.ds(start, blk)].astype(BF16)
        feat = lax.broadcasted_iota(jnp.int32, (LANES, 1), 0)
        one = jnp.ones(v.shape, BF16)
        vh = [jnp.where(feat < HEAD, v, one), jnp.where(feat < HEAD, one, v)]
        ss = [cur[i] for i in range(4)]
        ps = [jnp.exp2(s) for s in ss]
        if diag:
            ps = [jnp.where(causal, p, 0.0) for p in ps]
        pvs = [lax.dot_general(ps[i].astype(BF16), vh[i // 2], _NT, preferred_element_type=F32)
               for i in range(4)]
        for i in range(4):
            if diag:
                acc_ref[i] = pvs[i]
            else:
                acc_ref[i] += pvs[i]

    def tile_shifted(j, cur, nxt, diag):
        start = pl.multiple_of(j * blk, blk)
        k = k_ref[:, pl.ds(start, blk)].astype(BF16)
        v = v_ref[:, pl.ds(start, blk)].astype(BF16)
        for i in range(4):
            s = jnp.dot(qm[i], k, preferred_element_type=F32)
            if diag:
                s = jnp.where(causal, s, NEG_INF)
                m_new = jnp.max(s, axis=-1, keepdims=True)
                p = jnp.exp2(s - m_new)
                l_ref[i] = jnp.sum(p, axis=-1, keepdims=True)
                acc_ref[i] = lax.dot_general(p.astype(BF16), v, _NT, preferred_element_type=F32)
            else:
                m_old = m_ref[i]
                m_new = jnp.maximum(m_old, jnp.max(s, axis=-1, keepdims=True))
                a = jnp.exp2(m_old - m_new)
                p = jnp.exp2(s - m_new)
                l_ref[i] = a * l_ref[i] + jnp.sum(p, axis=-1, keepdims=True)
                acc_ref[i] = a * acc_ref[i] + lax.dot_general(p.astype(BF16), v, _NT,
                                                              preferred_element_type=F32)
            m_ref[i] = m_new

    _sweep_blocks(qi, scores, tile_shifted if shifted else tile_raw, bufs)
    lam = _diff_lambda(lam_ref, lam_init)
    if shifted:
        den = [l_ref[i] for i in range(4)]
    else:
        den = [pltpu.roll(acc_ref[i], HEAD, 1) for i in range(4)]
    o0 = acc_ref[0] / den[0] - lam * (acc_ref[1] / den[1])
    o1 = acc_ref[2] / den[2] - lam * (acc_ref[3] / den[3])
    o = jnp.where(lane < HEAD, o0, o1)
    o_ref[...] = _head_rms(o, gs_ref[...], lane < HEAD, 1.0 - lam_init)


def _df_prompt(lam_rows, gs2, q, k, v, n_seq, lam_init, shifted):
    T = q.shape[0]
    S = T // n_seq
    blk = min(ATT_BLOCK, S)
    nq = S // blk
    pairs = MIX // LANES
    qspec = pl.BlockSpec((blk, LANES), lambda b, p, i: (b * nq + i, p))
    kspec = pl.BlockSpec((None, LANES, S), lambda b, p, i: (b, p, 0))
    stats = [pltpu.VMEM((4, blk, 1), F32)] * 2 if shifted else [pltpu.VMEM((4, blk, blk), F32)] * 2
    return pl.pallas_call(
        functools.partial(_df_prompt_body, blk=blk, scale=DF_QK ** -0.5, lam_init=lam_init, shifted=shifted),
        grid=(n_seq, pairs, nq),
        in_specs=[_full(lam_rows.shape), _full((1, LANES)), qspec, kspec, kspec],
        out_specs=qspec,
        out_shape=jax.ShapeDtypeStruct((T, MIX), F32),
        scratch_shapes=[pltpu.VMEM((4, blk, LANES), F32)] + stats,
        compiler_params=_params(("parallel", "parallel", "arbitrary")),
        name="df_prompt_shifted" if shifted else "df_prompt",
    )(lam_rows, gs2, q, k, v)


def _merge_body(x_ref, g_ref, oa_ref, ob_ref, oc_ref, od_ref, wg_ref, wb_ref, wo_ref, out_ref):
    x = x_ref[...]
    D = x.shape[1]
    h = _rms(x, g_ref[...]).astype(BF16)
    acc = None
    for i, o_ref in enumerate((oa_ref, ob_ref, oc_ref, od_ref)):
        gate = jnp.dot(h, wg_ref[:, i * D:(i + 1) * D], preferred_element_type=F32)
        branch = jnp.dot(o_ref[...].astype(BF16), wb_ref[i], preferred_element_type=F32)
        t = branch / (1.0 + jnp.exp(-gate))
        acc = t if acc is None else acc + t
    out_ref[...] = x + jnp.dot(acc.astype(BF16), wo_ref[...], preferred_element_type=F32)


def _merge(x, g, oa, ob, oc, od, wg, wb, wo):
    T, D = x.shape
    tm = min(ROW_TILE, T)
    row = pl.BlockSpec((tm, D), lambda i: (i, 0))
    mix = pl.BlockSpec((tm, MIX), lambda i: (i, 0))
    return pl.pallas_call(
        _merge_body,
        grid=(T // tm,),
        in_specs=[row, _full((1, D)), mix, mix, mix, mix, _full(wg.shape), _full(wb.shape), _full(wo.shape)],
        out_specs=row,
        out_shape=jax.ShapeDtypeStruct((T, D), F32),
        compiler_params=_params(("parallel",)),
        name="merge",
    )(x, g, oa, ob, oc, od, wg, wb, wo)


def _memkv_body(m_ref, wt_ref, g_ref, bd_ref, k_ref, v_ref):
    yt = lax.dot_general(wt_ref[...], m_ref[...].astype(BF16), _NT, preferred_element_type=F32)
    k = yt[:MIX]
    ms = jnp.dot(bd_ref[...], (k * k).astype(BF16), preferred_element_type=F32)
    k_ref[...] = k * lax.rsqrt(ms + EPS) * g_ref[...]
    v_ref[...] = yt[MIX:]


def _memkv(mem, wt, g_col, bd64, n_seq):
    T, D = mem.shape
    M = T // n_seq
    out = pl.BlockSpec((None, MIX, M), lambda b: (b, 0, 0))
    return pl.pallas_call(
        _memkv_body,
        grid=(n_seq,),
        in_specs=[pl.BlockSpec((M, D), lambda b: (b, 0)), _full(wt.shape), _full((MIX, 1)), _full((MIX, MIX))],
        out_specs=[out, out],
        out_shape=[jax.ShapeDtypeStruct((n_seq, MIX, M), F32)] * 2,
        compiler_params=_params(("parallel",)),
        name="mem_kv",
    )(mem, wt, g_col, bd64)


def _cross_query(x, g, wq, gq, bd):
    h = _rms(x, g).astype(BF16)
    q = jnp.dot(h, wq, preferred_element_type=F32)
    return _group_rms(q, gq, bd) * HEAD ** -0.5


def _cross_heads(q, mk, mv):
    lane = lax.broadcasted_iota(jnp.int32, (1, LANES), 1)
    outs = []
    for p in range(MIX // LANES):
        qp = q[:, p * LANES:(p + 1) * LANES]
        kp = mk[p * LANES:(p + 1) * LANES, :].astype(BF16)
        vp = mv[p * LANES:(p + 1) * LANES, :].astype(BF16)
        oh = []
        for h in range(2):
            in_head = (lane >= h * HEAD) & (lane < (h + 1) * HEAD)
            s = jnp.dot(jnp.where(in_head, qp, 0.0).astype(BF16), kp, preferred_element_type=F32)
            e = jnp.exp(s - jnp.max(s, axis=-1, keepdims=True))
            o = lax.dot_general(e.astype(BF16), vp, _NT, preferred_element_type=F32)
            oh.append(o / jnp.sum(e, axis=-1, keepdims=True))
        outs.append(jnp.where(lane < HEAD, oh[0], oh[1]))
    return outs


def _cross_prompt_body(x_ref, g_ref, wq_ref, gq_ref, bd_ref, mk_ref, mv_ref, wo_ref, out_ref):
    x = x_ref[...]
    q = _cross_query(x, g_ref[...], wq_ref[...], gq_ref[...], bd_ref[...])
    o = jnp.concatenate(_cross_heads(q, mk_ref[...], mv_ref[...]), axis=-1)
    out_ref[...] = x + jnp.dot(o.astype(BF16), wo_ref[...], preferred_element_type=F32)


def _cross_prompt(x, g, wq, gq, bd64, mk, mv, wo, n_seq):
    T, D = x.shape
    S = T // n_seq
    M = mk.shape[2]
    tm = min(ROW_TILE, S)
    nt = S // tm
    row = pl.BlockSpec((tm, D), lambda b, i: (b * nt + i, 0))
    mem = pl.BlockSpec((None, MIX, M), lambda b, i: (b, 0, 0))
    return pl.pallas_call(
        _cross_prompt_body,
        grid=(n_seq, nt),
        in_specs=[row, _full((1, D)), _full(wq.shape), _full((1, MIX)), _full((MIX, MIX)), mem, mem,
                  _full(wo.shape)],
        out_specs=row,
        out_shape=jax.ShapeDtypeStruct((T, D), F32),
        compiler_params=_params(("parallel", "parallel")),
        name="cross_prompt",
    )(x, g, wq, gq, bd64, mk, mv, wo)


def _cross_sample_body(x_ref, g_ref, wq_ref, gq_ref, bd_ref, mk_ref, mv_ref, wo_ref, out_ref, q_scr, acc_scr,
                       *, nb, ts):
    j = pl.program_id(0)

    @pl.when(j == 0)
    def _():
        q_scr[...] = _cross_query(x_ref[...], g_ref[...], wq_ref[...], gq_ref[...], bd_ref[...])
        acc_scr[...] = jnp.zeros(acc_scr.shape, F32)

    rows = lax.broadcasted_iota(jnp.int32, (nb * ts, 1), 0)
    mine = rows == j
    for t in range(1, ts):
        mine = mine | (rows == t * nb + j)
    outs = _cross_heads(q_scr[...], mk_ref[...], mv_ref[...])
    for p, o in enumerate(outs):
        sl = slice(p * LANES, (p + 1) * LANES)
        acc_scr[:, sl] = jnp.where(mine, o, acc_scr[:, sl])

    @pl.when(j == nb - 1)
    def _():
        out_ref[...] = x_ref[...] + jnp.dot(acc_scr[...].astype(BF16), wo_ref[...], preferred_element_type=F32)


def _cross_sample(x, g, wq, gq, bd64, cache_k, cache_v, layer, wo, nb, ts):
    T, D = x.shape
    M = cache_k.shape[3]
    mem = pl.BlockSpec((None, None, MIX, M), lambda j: (layer, j, 0, 0))
    return pl.pallas_call(
        functools.partial(_cross_sample_body, nb=nb, ts=ts),
        grid=(nb,),
        in_specs=[_full((T, D)), _full((1, D)), _full(wq.shape), _full((1, MIX)), _full((MIX, MIX)), mem, mem,
                  _full(wo.shape)],
        out_specs=_full((T, D)),
        out_shape=jax.ShapeDtypeStruct((T, D), F32),
        scratch_shapes=[pltpu.VMEM((T, MIX), F32), pltpu.VMEM((T, MIX), F32)],
        compiler_params=_params(("arbitrary",)),
        name="cross_sample",
    )(x, g, wq, gq, bd64, cache_k, cache_v, wo)


def _conv_gate(a0, a1, a2, b, cw):
    ac = cw[3:4] + cw[0:1] * a2 + cw[1:2] * a1 + cw[2:3] * a0
    return ac / (1.0 + jnp.exp(-ac)) * b


def _ffn_prompt_body(x_ref, xh_ref, g_ref, wup_ref, cw_ref, wdn_ref, out_ref, cst_ref, h_scr, a_scr, b_scr,
                     acc_scr, *, tm, F, tf, tiles_per_seq):
    halo = BF16_ROWS
    first = (pl.program_id(0) % tiles_per_seq) == 0
    x = x_ref[...]
    h_scr[0:halo, :] = _rms(xh_ref[...], g_ref[...]).astype(BF16)
    h_scr[halo:halo + tm, :] = _rms(x, g_ref[...]).astype(BF16)
    n = F // tf

    def up(j, slot):
        a = jnp.dot(h_scr[...], wup_ref[:, j * tf:(j + 1) * tf], preferred_element_type=F32)
        a_scr[slot] = a
        a_scr[slot, 0:halo, :] = jnp.where(first, 0.0, a[0:halo])
        b_scr[slot] = jnp.dot(h_scr[halo:halo + tm, :], wup_ref[:, F + j * tf:F + (j + 1) * tf],
                              preferred_element_type=F32)

    up(0, 0)
    for j in range(n):
        slot = j % 2
        cols = slice(j * tf, (j + 1) * tf)
        if j + 1 < n:
            up(j + 1, 1 - slot)
        act = _conv_gate(a_scr[slot, halo:halo + tm, :], a_scr[slot, halo - 1:halo - 1 + tm, :],
                         a_scr[slot, halo - 2:halo - 2 + tm, :], b_scr[slot], cw_ref[:, cols])
        contrib = jnp.dot(act.astype(BF16), wdn_ref[cols, :], preferred_element_type=F32)
        if j == 0:
            acc_scr[...] = contrib
        else:
            acc_scr[...] += contrib
        cst_ref[0, :, cols] = a_scr[slot, tm:tm + halo, :]
    out_ref[...] = x + acc_scr[...]


def _ffn_prompt(x, g, wup, cw, wdn, n_seq):
    T, D = x.shape
    F = wdn.shape[0]
    S = T // n_seq
    tm = min(ROW_TILE, S)
    nt = S // tm
    halo = BF16_ROWS
    tf = FFN_CHUNK if F % FFN_CHUNK == 0 else LANES
    row = pl.BlockSpec((tm, D), lambda i: (i, 0))
    prev = pl.BlockSpec((halo, D), lambda i: (jnp.maximum(i * (tm // halo) - 1, 0), 0))
    return pl.pallas_call(
        functools.partial(_ffn_prompt_body, tm=tm, F=F, tf=tf, tiles_per_seq=nt),
        grid=(T // tm,),
        in_specs=[row, prev, _full((1, D)), _full(wup.shape), _full(cw.shape), _full(wdn.shape)],
        out_specs=[row, pl.BlockSpec((1, halo, F), lambda i: (i // nt, 0, 0))],
        out_shape=[jax.ShapeDtypeStruct((T, D), F32), jax.ShapeDtypeStruct((n_seq, halo, F), F32)],
        scratch_shapes=[pltpu.VMEM((halo + tm, D), BF16), pltpu.VMEM((2, halo + tm, tf), F32),
                        pltpu.VMEM((2, tm, tf), F32), pltpu.VMEM((tm, D), F32)],
        compiler_params=_params(("arbitrary",)),
        name="ffn_prompt",
    )(x, x, g, wup, cw, wdn)


def _ffn_sample_body(x_ref, prev_ref, g_ref, wup_ref, cw_ref, wdn_ref, out_ref, cst_ref, *, nb, F, tf):
    x = x_ref[...]
    T = x.shape[0]
    h = _rms(x, g_ref[...]).astype(BF16)
    acc = None
    for j in range(F // tf):
        cols = slice(j * tf, (j + 1) * tf)
        a = jnp.dot(h, wup_ref[:, cols], preferred_element_type=F32)
        b = jnp.dot(h, wup_ref[:, F + j * tf:F + (j + 1) * tf], preferred_element_type=F32)
        prev = prev_ref[:, cols]
        a1 = jnp.concatenate([prev[nb:2 * nb], a[:T - nb]], axis=0)
        a2 = jnp.concatenate([prev, a[:T - 2 * nb]], axis=0)
        act = _conv_gate(a, a1, a2, b, cw_ref[:, cols])
        contrib = jnp.dot(act.astype(BF16), wdn_ref[cols, :], preferred_element_type=F32)
        acc = contrib if acc is None else acc + contrib
        cst_ref[:, cols] = a[T - 2 * nb:]
    out_ref[...] = x + acc


def _ffn_sample(x, prev, g, wup, cw, wdn, nb):
    T, D = x.shape
    F = wdn.shape[0]
    tf = FFN_CHUNK if F % FFN_CHUNK == 0 else LANES
    return pl.pallas_call(
        functools.partial(_ffn_sample_body, nb=nb, F=F, tf=tf),
        grid=(1,),
        in_specs=[_full((T, D)), _full(prev.shape), _full((1, D)), _full(wup.shape), _full(cw.shape),
                  _full(wdn.shape)],
        out_specs=[_full((T, D)), _full(prev.shape)],
        out_shape=[jax.ShapeDtypeStruct((T, D), F32), jax.ShapeDtypeStruct(prev.shape, F32)],
        compiler_params=_params(("arbitrary",)),
        name="ffn_sample",
    )(x, prev, g, wup, cw, wdn)


def _local_sample_body(xb_ref, st_ref, uc_ref, vn_ref, wp_ref, sp_ref, coef_ref, bias_ref, ob_ref, oc_ref,
                       *, nb, ts, n_past):
    wl = _lane_window()
    ext = jnp.concatenate([st_ref[...], xb_ref[...]], axis=0)
    base = POOL_STATE * nb
    x = xb_ref[...]
    acc = x
    for k in range(1, POOL_STATE + 1):
        acc = acc + jnp.where(k < wl, ext[base - k * nb:base - k * nb + ts * nb], 0.0)
    for t in range(ts):
        rows = slice(t * nb, (t + 1) * nb)
        cnt = jnp.minimum(n_past + t + 1, wl).astype(F32)
        d = acc[rows] / cnt - x[rows]
        y = jnp.dot(d.astype(BF16), wp_ref[...], preferred_element_type=F32)
        ob_ref[rows, :] = y * sp_ref[...]
        mixed = bias_ref[t:t + 1, :]
        for s in range(t + 1):
            mixed = mixed + coef_ref[t * ts + s:t * ts + s + 1, :] * vn_ref[s * nb:(s + 1) * nb, :]
        oc_ref[rows, :] = uc_ref[rows, :] * mixed


def _local_sample(xb, state, uc, vn, wp_bd, sp, coef, bias, nb, ts, n_past):
    T = xb.shape[0]
    return pl.pallas_call(
        functools.partial(_local_sample_body, nb=nb, ts=ts, n_past=n_past),
        grid=(1,),
        in_specs=[_full((T, MIX)), _full(state.shape), _full((T, MIX)), _full((T, MIX)), _full((MIX, MIX)),
                  _full((1, MIX)), _full(coef.shape), _full(bias.shape)],
        out_specs=[_full((T, MIX))] * 2,
        out_shape=[jax.ShapeDtypeStruct((T, MIX), F32)] * 2,
        compiler_params=_params(("arbitrary",)),
        name="local_sample",
    )(xb, state, uc, vn, wp_bd, sp, coef, bias)


Q_PAD = 8


def _slot_queries(q8, slot_width):
    lane = lax.broadcasted_iota(jnp.int32, (1, MIX), 1)
    slots = [jnp.where((lane >= i * slot_width) & (lane < (i + 1) * slot_width), q8, 0.0)
             for i in range(MIX // slot_width)]
    return jnp.concatenate(slots, axis=0).astype(BF16)


def _paged_fetch(pt_ref, cache_k, cache_v, kbuf, vbuf, sem, *, layer, G, reverse, nb, nj):
    slots = kbuf.shape[0]
    ahead = slots - 1
    step = pl.program_id(0) * nj + pl.program_id(1)

    def copies(t):
        bb, jj = t // nj, t % nj
        group = nj - 1 - jj if reverse else jj
        sl = t % slots
        out = []
        for g in range(G):
            pid = pt_ref[bb, group * G + g]
            out.append(pltpu.make_async_copy(cache_k.at[layer, pid], kbuf.at[sl, g], sem.at[sl, 0, g]))
            out.append(pltpu.make_async_copy(cache_v.at[layer, pid], vbuf.at[sl, g], sem.at[sl, 1, g]))
        return out

    @pl.when(step == 0)
    def _():
        for t in range(min(ahead, nb * nj)):
            for c in copies(t):
                c.start()

    @pl.when(step + ahead < nb * nj)
    def _():
        for c in copies(step + ahead):
            c.start()

    for c in copies(step):
        c.wait()
    slot = step % slots
    return [kbuf.at[slot, g] for g in range(G)], [vbuf.at[slot, g] for g in range(G)]


def _segment(page_refs, s, pages_per_seg):
    parts = [page_refs[s * pages_per_seg + g][...].astype(BF16) for g in range(pages_per_seg)]
    return parts[0] if pages_per_seg == 1 else jnp.concatenate(parts, axis=1)


def _sb_decode_body(pt_ref, q_ref, kn_ref, vn_ref, ck_ref, cv_ref, o_ref, acc_ref, car_ref, kbuf, vbuf, sem,
                    *, G, scale, layer, nb, nj):
    k_refs, v_refs = _paged_fetch(pt_ref, ck_ref, cv_ref, kbuf, vbuf, sem, layer=layer, G=G, reverse=True,
                                  nb=nb, nj=nj)
    j = pl.program_id(1)
    rows = (MIX // HEAD) * Q_PAD
    qbd = _slot_queries(q_ref[0] * (scale * LOG2_E), HEAD)

    def group(kbs, vbs, valid):
        pages = range(len(kbs))
        n = kbs[0].shape[1]
        upper = (lax.broadcasted_iota(jnp.int32, (n, n), 0) > lax.broadcasted_iota(jnp.int32, (n, n), 1))
        lss, lks = zip(*[_log2_sigmoids(jnp.dot(qbd, kb, preferred_element_type=F32)) for kb in kbs])
        if valid is not None:
            lks = [jnp.where(valid, lk, 0.0) for lk in lks]
        stacked = jnp.concatenate([lk.astype(BF16) for lk in lks], axis=0)
        bt_rows = jnp.dot(stacked, upper.astype(BF16), preferred_element_type=F32)
        bts = [bt_rows[p * rows:(p + 1) * rows] for p in pages]
        car = car_ref[...]
        cars = [None] * len(kbs)
        for p in reversed(pages):
            cars[p] = car
            car = car + (bts[p][:, 0:1] + lks[p][:, 0:1])
        pv = None
        for p in pages:
            w = jnp.exp2(lss[p] + bts[p] + cars[p])
            if valid is not None:
                w = jnp.where(valid, w, 0.0)
            d = lax.dot_general(w.astype(BF16), vbs[p], _NT, preferred_element_type=F32)
            pv = d if pv is None else pv + d
        acc_ref[...] += pv
        car_ref[...] = car

    @pl.when(j == 0)
    def _():
        acc_ref[...] = jnp.zeros(acc_ref.shape, F32)
        car_ref[...] = jnp.zeros(car_ref.shape, F32)
        n = kn_ref.shape[2]
        t = lax.broadcasted_iota(jnp.int32, (rows, n), 0) & (Q_PAD - 1)
        s = lax.broadcasted_iota(jnp.int32, (rows, n), 1)
        group([kn_ref[0].astype(BF16)], [vn_ref[0].astype(BF16)], s < t)

    group([ref[...].astype(BF16) for ref in k_refs], [ref[...].astype(BF16) for ref in v_refs], None)

    @pl.when(j == nj - 1)
    def _():
        lane = lax.broadcasted_iota(jnp.int32, (1, MIX), 1)
        o = jnp.zeros((Q_PAD, MIX), F32)
        for h in range(MIX // HEAD):
            in_head = (lane >= h * HEAD) & (lane < (h + 1) * HEAD)
            o = jnp.where(in_head, acc_ref[h * Q_PAD:(h + 1) * Q_PAD, :], o)
        o_ref[0] = o


def _df_decode_body(pt_ref, lam_ref, gs_ref, q_ref, kn_ref, vn_ref, ck_ref, cv_ref, o_ref, acc_ref, m_ref, l_ref,
                    kbuf, vbuf, sem, *, G, scale, lam_init, layer, nb, nj):
    k_refs, v_refs = _paged_fetch(pt_ref, ck_ref, cv_ref, kbuf, vbuf, sem, layer=layer, G=G, reverse=False,
                                  nb=nb, nj=nj)
    j = pl.program_id(1)
    slots = MIX // DF_QK
    rows = slots * Q_PAD
    qbd = _slot_queries(q_ref[0] * (scale * LOG2_E), DF_QK)
    page = k_refs[0].shape[1]
    seg = max(ATT_BLOCK, page)

    def block(kbs, vbs, valid):
        ss = [jnp.dot(qbd, kb, preferred_element_type=F32) for kb in kbs]
        if valid is not None:
            ss = [jnp.where(valid, s, NEG_INF) for s in ss]
        top = ss[0]
        for s in ss[1:]:
            top = jnp.maximum(top, s)
        m_old = m_ref[...]
        m_new = jnp.maximum(m_old, jnp.max(top, axis=-1, keepdims=True))
        a = jnp.exp2(m_old - m_new)
        ps = [jnp.exp2(s - m_new) for s in ss]
        psum = ps[0]
        for p in ps[1:]:
            psum = psum + p
        pv = None
        for p, vb in zip(ps, vbs):
            d = lax.dot_general(p.astype(BF16), vb, _NT, preferred_element_type=F32)
            pv = d if pv is None else pv + d
        l_ref[...] = a * l_ref[...] + jnp.sum(psum, axis=-1, keepdims=True)
        acc_ref[...] = a * acc_ref[...] + pv
        m_ref[...] = m_new

    @pl.when(j == 0)
    def _():
        acc_ref[...] = jnp.zeros(acc_ref.shape, F32)
        l_ref[...] = jnp.zeros(l_ref.shape, F32)
        m_ref[...] = jnp.full(m_ref.shape, NEG_INF, F32)
        n = kn_ref.shape[2]
        t = lax.broadcasted_iota(jnp.int32, (rows, n), 0) & (Q_PAD - 1)
        s = lax.broadcasted_iota(jnp.int32, (rows, n), 1)
        block([kn_ref[0].astype(BF16)], [vn_ref[0].astype(BF16)], s <= t)

    pps = seg // page
    nseg = G // pps
    block([_segment(k_refs, s, pps) for s in range(nseg)], [_segment(v_refs, s, pps) for s in range(nseg)], None)

    @pl.when(j == nj - 1)
    def _():
        lane = lax.broadcasted_iota(jnp.int32, (1, MIX), 1)
        lam = _diff_lambda(lam_ref, lam_init)
        norm = acc_ref[...] / l_ref[...]
        o = jnp.zeros((Q_PAD, MIX), F32)
        for h in range(MIX // HEAD):
            in_head = (lane >= h * HEAD) & (lane < (h + 1) * HEAD)
            r0 = (2 * h) * Q_PAD
            oh = norm[r0:r0 + Q_PAD, :] - lam * norm[r0 + Q_PAD:r0 + 2 * Q_PAD, :]
            o = jnp.where(in_head, oh, o)
        lane2 = lax.broadcasted_iota(jnp.int32, (1, LANES), 1)
        parts = [_head_rms(o[:, p * LANES:(p + 1) * LANES], gs_ref[...], lane2 < HEAD, 1.0 - lam_init)
                 for p in range(MIX // LANES)]
        o_ref[0] = jnp.concatenate(parts, axis=-1)


def _decode_specs(cache_k, n_pages, G):
    nj = n_pages // G
    page = cache_k.shape[3]
    tok = pl.BlockSpec((1, Q_PAD, MIX), lambda b, j, pt: (b, 0, 0))
    new = pl.BlockSpec((1, MIX, LANES), lambda b, j, pt: (b, 0, 0))
    hbm = pl.BlockSpec(memory_space=pl.ANY)
    fetch_scratch = [pltpu.VMEM((DECODE_SLOTS, G, MIX, page), F32), pltpu.VMEM((DECODE_SLOTS, G, MIX, page), F32),
                     pltpu.SemaphoreType.DMA((DECODE_SLOTS, 2, G))]
    return nj, tok, new, hbm, fetch_scratch


def _sb_decode(page_table, q, k_new, v_new, cache_k, cache_v, layer):
    nb, n_pages = page_table.shape
    G = PAGES_PER_STEP
    nj, tok, new, hbm, fetch_scratch = _decode_specs(cache_k, n_pages, G)
    rows = (MIX // HEAD) * Q_PAD
    grid_spec = pltpu.PrefetchScalarGridSpec(
        num_scalar_prefetch=1,
        grid=(nb, nj),
        in_specs=[tok, new, new, hbm, hbm],
        out_specs=tok,
        scratch_shapes=[pltpu.VMEM((rows, MIX), F32), pltpu.VMEM((rows, 1), F32)] + fetch_scratch,
    )
    return pl.pallas_call(
        functools.partial(_sb_decode_body, G=G, scale=HEAD ** -0.5, layer=layer, nb=nb, nj=nj),
        grid_spec=grid_spec,
        out_shape=jax.ShapeDtypeStruct((nb, Q_PAD, MIX), F32),
        compiler_params=_params(("arbitrary", "arbitrary")),
        name="sb_decode",
    )(page_table, q, k_new, v_new, cache_k, cache_v)


def _df_decode(page_table, lam_rows, gs2, q, k_new, v_new, cache_k, cache_v, layer, lam_init):
    nb, n_pages = page_table.shape
    G = PAGES_PER_STEP
    nj, tok, new, hbm, fetch_scratch = _decode_specs(cache_k, n_pages, G)
    rows = (MIX // DF_QK) * Q_PAD
    const = lambda shape: pl.BlockSpec(shape, lambda b, j, pt: (0,) * len(shape))
    grid_spec = pltpu.PrefetchScalarGridSpec(
        num_scalar_prefetch=1,
        grid=(nb, nj),
        in_specs=[const(lam_rows.shape), const((1, LANES)), tok, new, new, hbm, hbm],
        out_specs=tok,
        scratch_shapes=[pltpu.VMEM((rows, MIX), F32), pltpu.VMEM((rows, 1), F32), pltpu.VMEM((rows, 1), F32)]
        + fetch_scratch,
    )
    return pl.pallas_call(
        functools.partial(_df_decode_body, G=G, scale=DF_QK ** -0.5, lam_init=lam_init, layer=layer, nb=nb,
                          nj=nj),
        grid_spec=grid_spec,
        out_shape=jax.ShapeDtypeStruct((nb, Q_PAD, MIX), F32),
        compiler_params=_params(("arbitrary", "arbitrary")),
        name="df_decode",
    )(page_table, lam_rows, gs2, q, k_new, v_new, cache_k, cache_v)


def _row(v):
    return v.reshape(1, -1).astype(F32)


def _tile_row(v, width):
    return jnp.tile(v.astype(F32), width // v.shape[0]).reshape(1, width)


def _pad_rows(a, rows):
    return jnp.pad(a, ((0, rows - a.shape[0]),) + ((0, 0),) * (a.ndim - 1))


def _block_diag(w):
    G, c, d = w.shape
    eye = jnp.eye(G, dtype=w.dtype)
    return (eye[:, None, :, None] * w[:, :, None, :]).reshape(G * c, G * d)


def kernel(x_prompt, x_sample, mem_prompt, cache_sb_k, cache_sb_v, cache_df_k, cache_df_v, cache_mem_k, cache_mem_v, state_pool, state_ffn_conv, page_table, g_mix, w_in, w_branch, w_out, w_pool, s_pool, g_gm, w_spatial, b_spatial, g_qd, g_kd, lam_q1, lam_k1, lam_q2, lam_k2, g_subln, g_cross, w_cq, w_ckv, w_co, g_cq, g_ck, g_ffn, w_up, w_conv, b_conv, w_down):
    B, S, D = x_prompt.shape
    NB, TS, _ = x_sample.shape
    depth = w_in.shape[0]
    M = mem_prompt.shape[1]
    F = w_down.shape[1]
    page = cache_sb_k.shape[2]
    n_pages = page_table.shape[1]
    n_past = n_pages * page
    heads = MIX // HEAD
    assert TS >= CONV_W - 1 and TS <= Q_PAD and TS <= GM_CHUNK and n_pages % PAGES_PER_STEP == 0

    bd32 = _block_diag_mean(MIX, DF_QK)
    bd64 = _block_diag_mean(MIX, HEAD)
    feat_major = lambda c: c.transpose(0, 1, 3, 4, 2).reshape(c.shape[0], c.shape[1], MIX, c.shape[2])
    csb_k, csb_v, cdf_k, cdf_v = (feat_major(c) for c in (cache_sb_k, cache_sb_v, cache_df_k, cache_df_v))
    cmem_k, cmem_v = feat_major(cache_mem_k), feat_major(cache_mem_v)
    tok_major = lambda a: a.reshape(a.shape[0], heads, HEAD, a.shape[2]).transpose(0, 3, 1, 2)

    xp = x_prompt.reshape(B * S, D)
    xs = x_sample.transpose(1, 0, 2).reshape(TS * NB, D)
    mem = mem_prompt.reshape(B * M, D)

    to_batch = lambda a: a.reshape(TS, NB, MIX).transpose(1, 0, 2)
    to_time = lambda a: a[:, :TS].transpose(1, 0, 2).reshape(TS * NB, MIX)
    pad_tok = lambda a, n: jnp.pad(to_batch(a), ((0, 0), (0, n - TS), (0, 0)))
    new_page = lambda a: jnp.pad(to_batch(a).transpose(0, 2, 1), ((0, 0), (0, 0), (0, LANES - TS)))
    kv_cols = lambda w: jnp.concatenate([w[:, i * MIX:(i + 1) * MIX] for i in (1, 2, 7, 8)], axis=1)
    tok_cols = lambda w: jnp.concatenate([w[:, i * MIX:(i + 1) * MIX] for i in (0, 3, 4, 5, 6)], axis=1)

    outs = [[] for _ in range(15)]
    for l in range(depth):
        lam_init = 0.8 - 0.6 * math.exp(-0.3 * l)
        w_l = w_in[l]
        wtok = tok_cols(w_l).astype(BF16)
        wkv = kv_cols(w_l).astype(BF16)
        wkv_t = wkv.T
        wg = w_l[:, 9 * MIX:].astype(BF16)
        wb = w_branch[l].astype(BF16)
        wo = w_out[l].astype(BF16)
        gq, gk = _tile_row(g_qd[l], MIX), _tile_row(g_kd[l], MIX)
        gk_col = gk.reshape(MIX, 1)
        ggm, gmix = _row(g_gm[l]), _row(g_mix[l])
        wp_bd = _block_diag(w_pool[l]).astype(BF16)
        sp = _row(s_pool[l])
        bs_full = jnp.repeat(b_spatial[l].T, MIX // GM_GROUPS, axis=1).astype(F32)
        lam_rows = _pad_rows(jnp.stack([lam_q1[l], lam_k1[l], lam_q2[l], lam_k2[l]]).astype(F32), 8)
        gs2 = _tile_row(g_subln[l], LANES)
        wcq = w_cq[l].astype(BF16)
        wckv_t = w_ckv[l].astype(BF16).T
        wco = w_co[l].astype(BF16)
        gcq, gck_col = _tile_row(g_cq[l], MIX), _tile_row(g_ck[l], MIX).reshape(MIX, 1)
        wup = w_up[l].astype(BF16)
        wdn = w_down[l].astype(BF16)
        cw = _pad_rows(jnp.concatenate([w_conv[l], b_conv[l][None]], axis=0).astype(F32), 8)

        qa, xb, uc, vn, qd, ka, va, kd, vd = _proj_in(xp, gmix, wtok, wkv_t, gq, gk_col, ggm, bd32, B, True)
        oa = _sb_prompt(qa, ka, va, B)
        ob, oc = _local_prompt(xb, uc, vn, wp_bd, sp, w_spatial[l].astype(F32), bs_full, B)
        raw_exp_ok = (DF_QK ** 0.5 * 1.01 * jnp.max(jnp.abs(g_qd[l])) * jnp.max(jnp.abs(g_kd[l]))
                      <= DF_RAW_EXP_BOUND)
        od = lax.cond(raw_exp_ok,
                      functools.partial(_df_prompt, n_seq=B, lam_init=lam_init, shifted=False),
                      functools.partial(_df_prompt, n_seq=B, lam_init=lam_init, shifted=True),
                      lam_rows, gs2, qd, kd, vd)
        xp = _merge(xp, gmix, oa, ob, oc, od, wg, wb, wo)
        mk, mv = _memkv(mem, wckv_t, gck_col, bd64, B)
        xp = _cross_prompt(xp, _row(g_cross[l]), wcq, gcq, bd64, mk, mv, wco, B)
        xp, cst = _ffn_prompt(xp, _row(g_ffn[l]), wup, cw, wdn, B)
        outs[0].append(tok_major(ka))
        outs[1].append(tok_major(va))
        outs[2].append(tok_major(kd))
        outs[3].append(tok_major(vd))
        outs[4].append(tok_major(mk))
        outs[5].append(tok_major(mv))
        outs[6].append(xb.reshape(B, S, MIX)[:, S - POOL_STATE:])
        outs[7].append(cst[:, -(CONV_W - 1):])

        qa, xb, uc, vn, qd, ka, va, kd, vd = _proj_in(xs, gmix, wtok, wkv, gq, gk, ggm, bd32, 1, False)
        oa = _sb_decode(page_table, pad_tok(qa, Q_PAD), new_page(ka), new_page(va), csb_k, csb_v, l)
        od = _df_decode(page_table, lam_rows, gs2, pad_tok(qd, Q_PAD), new_page(kd), new_page(vd),
                        cdf_k, cdf_v, l, lam_init)
        state_t = state_pool[l].astype(F32).transpose(1, 0, 2).reshape(POOL_STATE * NB, MIX)
        ws_t = jnp.where(jnp.tril(jnp.ones((TS, TS), bool)), w_spatial[l][:, :TS, :TS], 0.0)
        coef = _pad_rows(jnp.repeat(ws_t.transpose(1, 2, 0).reshape(TS * TS, GM_GROUPS),
                                    MIX // GM_GROUPS, axis=1).astype(F32), -(-TS * TS // 8) * 8)
        bias = _pad_rows(bs_full[:TS], 8)
        ob, oc = _local_sample(xb, state_t, uc, vn, wp_bd, sp, coef, bias, NB, TS, n_past)
        xs = _merge(xs, gmix, to_time(oa), ob, oc, to_time(od), wg, wb, wo)
        xs = _cross_sample(xs, _row(g_cross[l]), wcq, gcq, bd64, cmem_k, cmem_v, l, wco, NB, TS)
        prev_t = state_ffn_conv[l].astype(F32).transpose(1, 0, 2).reshape((CONV_W - 1) * NB, F)
        xs, cst_s = _ffn_sample(xs, prev_t, _row(g_ffn[l]), wup, cw, wdn, NB)
        heads_s = lambda a: to_batch(a).reshape(NB, TS, heads, HEAD)
        outs[8].append(heads_s(ka))
        outs[9].append(heads_s(va))
        outs[10].append(heads_s(kd))
        outs[11].append(heads_s(vd))
        pool_ext = jnp.concatenate([state_pool[l].astype(F32), to_batch(xb)], axis=1)
        outs[12].append(pool_ext[:, -POOL_STATE:])
        outs[13].append(cst_s.reshape(CONV_W - 1, NB, F).transpose(1, 0, 2))
        outs[14].append(to_batch(vn))

    y_prompt = xp.reshape(B, S, D)
    y_sample = xs.reshape(TS, NB, D).transpose(1, 0, 2)
    return (y_prompt, y_sample) + tuple(jnp.stack(o) for o in outs)
```

```python
import functools
import math

import numpy as np
import jax
import jax.numpy as jnp
from jax import lax
from jax.experimental import pallas as pl
from jax.experimental.pallas import tpu as pltpu

F32 = jnp.float32
BF16 = jnp.bfloat16

EPS = 1e-6
NEG_INF = -1e30
LOG2_E = math.log2(math.e)
MIX = 256
HEAD = 64
DF_QK = 32
POOL_WINDOWS = (2, 4, 8, 16)
POOL_STATE = max(POOL_WINDOWS) - 1
POOL_PAD = 16
GM_CHUNK = 128
GM_GROUPS = 4
CONV_W = 3
LANES = 128
BF16_ROWS = 16
ROW_TILE = 512
ATT_BLOCK = 256
FFN_CHUNK = 256
PAGES_PER_STEP = 16
DECODE_SLOTS = 3
DF_RAW_EXP_BOUND = 40.0
VMEM_LIMIT = 56 * 1024 * 1024

_NT = (((1,), (1,)), ((), ()))


def _params(sem):
    return pltpu.CompilerParams(dimension_semantics=sem, vmem_limit_bytes=VMEM_LIMIT)


def _rms(x, g):
    return x * lax.rsqrt(jnp.mean(x * x, axis=-1, keepdims=True) + EPS) * g


def _group_rms(v, g, bd):
    ms = jnp.dot((v * v).astype(BF16), bd, preferred_element_type=F32)
    return v * lax.rsqrt(ms + EPS) * g


def _block_diag_mean(width, group):
    i = np.arange(width)
    return jnp.asarray((i[:, None] // group == i[None, :] // group) / group, BF16)


def _full(shape):
    return pl.BlockSpec(shape, lambda *_: (0,) * len(shape))


def _lane_window():
    lane = lax.broadcasted_iota(jnp.int32, (1, MIX), 1)
    gdim = MIX // len(POOL_WINDOWS)
    wl = jnp.full((1, MIX), POOL_WINDOWS[-1], jnp.int32)
    for g in range(len(POOL_WINDOWS) - 2, -1, -1):
        wl = jnp.where(lane < (g + 1) * gdim, POOL_WINDOWS[g], wl)
    return wl


def _proj_in_body(x_ref, g_ref, wtok_ref, wkv_ref, gq_ref, gk_ref, ggm_ref, bd_ref, *rest, kv_major, n_prev):
    prev_refs, outs = rest[:len(rest) - 9], rest[len(rest) - 9:]
    qa_ref, xb_ref, uc_ref, vn_ref, qd_ref, ka_ref, va_ref, kd_ref, vd_ref = outs
    h = _rms(x_ref[...], g_ref[...]).astype(BF16)
    y = jnp.dot(h, wtok_ref[...], preferred_element_type=F32)
    part = lambda i: y[:, i * MIX:(i + 1) * MIX]
    qa_ref[...] = part(0)
    xb_ref[...] = part(1)
    uc_ref[...] = part(2)
    vn_ref[...] = _rms(part(3), ggm_ref[...])
    qd_ref[...] = _group_rms(part(4), gq_ref[...], bd_ref[...])
    if kv_major:
        yt = lax.dot_general(wkv_ref[...], h, _NT, preferred_element_type=F32)
        kd = yt[2 * MIX:3 * MIX]
        ms = jnp.dot(bd_ref[...], (kd * kd).astype(BF16), preferred_element_type=F32)
        new = (yt[0:MIX], yt[MIX:2 * MIX], kd * lax.rsqrt(ms + EPS) * gk_ref[...], yt[3 * MIX:4 * MIX])
        for i, out_ref in enumerate((ka_ref, va_ref, kd_ref, vd_ref)):
            if n_prev:
                out_ref[0:n_prev] = prev_refs[i][...]
            out_ref[n_prev] = new[i]
    else:
        y2 = jnp.dot(h, wkv_ref[...], preferred_element_type=F32)
        ka_ref[...] = y2[:, 0:MIX]
        va_ref[...] = y2[:, MIX:2 * MIX]
        kd_ref[...] = _group_rms(y2[:, 2 * MIX:3 * MIX], gk_ref[...], bd_ref[...])
        vd_ref[...] = y2[:, 3 * MIX:4 * MIX]


def _proj_in(x, g, wtok, wkv, gq, gk, ggm, bd32, n_seq, kv_major, prev=()):
    T, D = x.shape
    S = T // n_seq
    tm = min(ROW_TILE, S)
    nt = S // tm
    row = pl.BlockSpec((tm, D), lambda i: (i, 0))
    out = pl.BlockSpec((tm, MIX), lambda i: (i, 0))
    tok_shape = jax.ShapeDtypeStruct((T, MIX), F32)
    n_prev = prev[0].shape[0] if prev else 0
    slabs = lambda n: pl.BlockSpec((n, None, MIX, tm), lambda i: (0, i // nt, 0, i % nt))
    if kv_major:
        kv_spec = slabs(n_prev + 1)
        kv_shape = jax.ShapeDtypeStruct((n_prev + 1, n_seq, MIX, S), F32)
    else:
        kv_spec, kv_shape = out, tok_shape
    return pl.pallas_call(
        functools.partial(_proj_in_body, kv_major=kv_major, n_prev=n_prev),
        grid=(T // tm,),
        in_specs=[row, _full((1, D)), _full(wtok.shape), _full(wkv.shape), _full((1, MIX)), _full(gk.shape),
                  _full((1, MIX)), _full((MIX, MIX))] + [slabs(n_prev)] * len(prev),
        out_specs=[out] * 5 + [kv_spec] * 4,
        out_shape=[tok_shape] * 5 + [kv_shape] * 4,
        compiler_params=_params(("parallel",)),
        name="proj_in",
    )(x, g, wtok, wkv, gq, gk, ggm, bd32, *prev)


def _local_prompt_body(xb_ref, uc_ref, vn_ref, wp_ref, sp_ref, ws_ref, bs_ref, ob_ref, oc_ref, ext_ref):
    S = xb_ref.shape[0]
    ext_ref[0:POOL_PAD, :] = jnp.zeros((POOL_PAD, MIX), F32)
    ext_ref[POOL_PAD:POOL_PAD + S, :] = xb_ref[...]
    wl = _lane_window()
    lane = lax.broadcasted_iota(jnp.int32, (1, MIX), 1)
    R = min(256, S)
    for c in range(S // R):
        x = xb_ref[c * R:(c + 1) * R, :]
        acc = x
        for k in range(1, POOL_STATE + 1):
            lo = POOL_PAD + c * R - k
            acc = acc + jnp.where(k < wl, ext_ref[lo:lo + R, :], 0.0)
        pos = c * R + lax.broadcasted_iota(jnp.int32, (R, 1), 0)
        cnt = jnp.minimum(pos + 1, wl).astype(F32)
        d = acc / cnt - x
        y = jnp.dot(d.astype(BF16), wp_ref[...], preferred_element_type=F32)
        ob_ref[c * R:(c + 1) * R, :] = y * sp_ref[...]

    C = min(GM_CHUNK, S)
    r = lax.broadcasted_iota(jnp.int32, (C, C), 0)
    cidx = lax.broadcasted_iota(jnp.int32, (C, C), 1)
    gdim = MIX // GM_GROUPS
    ws = [jnp.where(r >= cidx, ws_ref[g][:C, :C], 0.0).astype(BF16) for g in range(GM_GROUPS)]
    bias = bs_ref[0:C, :]
    for n in range(S // C):
        vn = vn_ref[n * C:(n + 1) * C, :].astype(BF16)
        mixed = jnp.dot(ws[0], vn, preferred_element_type=F32)
        for g in range(1, GM_GROUPS):
            mixed = jnp.where(lane >= g * gdim, jnp.dot(ws[g], vn, preferred_element_type=F32), mixed)
        oc_ref[n * C:(n + 1) * C, :] = uc_ref[n * C:(n + 1) * C, :] * (mixed + bias)


def _local_prompt(xb, uc, vn, wp_bd, sp, ws, bs_full, n_seq):
    T = xb.shape[0]
    S = T // n_seq
    seq = pl.BlockSpec((S, MIX), lambda b: (b, 0))
    return pl.pallas_call(
        _local_prompt_body,
        grid=(n_seq,),
        in_specs=[seq, seq, seq, _full((MIX, MIX)), _full((1, MIX)), _full(ws.shape), _full(bs_full.shape)],
        out_specs=[seq, seq],
        out_shape=[jax.ShapeDtypeStruct((T, MIX), F32)] * 2,
        scratch_shapes=[pltpu.VMEM((POOL_PAD + S, MIX), F32)],
        compiler_params=_params(("parallel",)),
        name="local_prompt",
    )(xb, uc, vn, wp_bd, sp, ws, bs_full)


def _log_sigmoids(z):
    ls = jnp.minimum(z, 0.0) - jnp.log(1.0 + jnp.exp(-jnp.abs(z)))
    return ls, ls - z


def _suffix_sums(lk, upper):
    hi = lk.astype(BF16)
    lo = (lk - hi.astype(F32)).astype(BF16)
    return (jnp.dot(hi, upper, preferred_element_type=F32)
            + jnp.dot(lo, upper, preferred_element_type=F32))


def _log2_sigmoids(z2):
    ls2 = jnp.minimum(z2, 0.0) - jnp.log2(1.0 + jnp.exp2(-jnp.abs(z2)))
    return ls2, ls2 - z2


def _sweep_blocks(qi, scores, tile, bufs):
    a, b = bufs
    scores(qi, a)
    tile(qi, a, b, True)

    def pair(p, carry):
        j = qi - 1 - 2 * p
        tile(j, b, a, False)
        tile(j - 1, a, b, False)
        return carry

    lax.fori_loop(0, qi // 2, pair, 0)

    @pl.when(qi % 2 == 1)
    def _():
        tile(0, b, a, False)


def _sb_prompt_body(q_ref, k_ref, v_ref, o_ref, acc_ref, car_ref, za_ref, zb_ref, *, blk, scale):
    qi = pl.program_id(2)
    lane = lax.broadcasted_iota(jnp.int32, (1, LANES), 1)
    q = q_ref[...] * (scale * LOG2_E)
    qh = [jnp.where(lane < HEAD, q, 0.0).astype(BF16), jnp.where(lane >= HEAD, q, 0.0).astype(BF16)]
    r = lax.broadcasted_iota(jnp.int32, (blk, blk), 0)
    c = lax.broadcasted_iota(jnp.int32, (blk, blk), 1)
    upper = (r > c).astype(BF16)
    causal = c < r

    heads = range(2)

    def scores(j, buf):
        start = pl.multiple_of(j * blk, blk)
        k = k_ref[:, pl.ds(start, blk)].astype(BF16)
        for h in heads:
            buf[h] = jnp.dot(qh[h], k, preferred_element_type=F32)

    def tile(j, cur, nxt, diag):
        scores(jnp.maximum(j - 1, 0), nxt)
        start = pl.multiple_of(j * blk, blk)
        v = v_ref[:, pl.ds(start, blk)].astype(BF16)
        zs = [cur[h] for h in heads]
        lss, lks = zip(*[_log2_sigmoids(z) for z in zs])
        if diag:
            lks = [jnp.where(causal, lk, 0.0) for lk in lks]
        bts = [jnp.dot(lks[h].astype(BF16), upper, preferred_element_type=F32) for h in heads]
        ws = [jnp.exp2(lss[h] + bts[h]) for h in heads]
        if diag:
            ws = [jnp.where(causal, w, 0.0) for w in ws]
        pvs = [lax.dot_general(ws[h].astype(BF16), v, _NT, preferred_element_type=F32) for h in heads]
        for h in heads:
            tot = bts[h][:, 0:1] + lks[h][:, 0:1]
            if diag:
                acc_ref[h] = pvs[h]
                car_ref[h] = tot
            else:
                acc_ref[h] += jnp.exp2(car_ref[h]) * pvs[h]
                car_ref[h] += tot

    _sweep_blocks(qi, scores, tile, (za_ref, zb_ref))
    o_ref[...] = jnp.where(lane < HEAD, acc_ref[0], acc_ref[1])


def _sb_prompt(q, k, v, n_seq):
    T = q.shape[0]
    S = T // n_seq
    blk = min(ATT_BLOCK, S)
    nq = S // blk
    pairs = MIX // LANES
    qspec = pl.BlockSpec((blk, LANES), lambda b, p, i: (b * nq + i, p))
    kspec = pl.BlockSpec((None, None, LANES, S), lambda b, p, i: (k.shape[0] - 1, b, p, 0))
    return pl.pallas_call(
        functools.partial(_sb_prompt_body, blk=blk, scale=HEAD ** -0.5),
        grid=(n_seq, pairs, nq),
        in_specs=[qspec, kspec, kspec],
        out_specs=qspec,
        out_shape=jax.ShapeDtypeStruct((T, MIX), F32),
        scratch_shapes=[pltpu.VMEM((2, blk, LANES), F32), pltpu.VMEM((2, blk, 1), F32),
                        pltpu.VMEM((2, blk, blk), F32), pltpu.VMEM((2, blk, blk), F32)],
        compiler_params=_params(("parallel", "parallel", "arbitrary")),
        name="sb_prompt",
    )(q, k, v)


def _diff_lambda(lam_ref, lam_init):
    s1 = jnp.sum(lam_ref[0:1, :] * lam_ref[1:2, :], axis=-1, keepdims=True)
    s2 = jnp.sum(lam_ref[2:3, :] * lam_ref[3:4, :], axis=-1, keepdims=True)
    return jnp.exp(s1) - jnp.exp(s2) + lam_init


def _head_rms(o, g, lane_in_head0, out_scale):
    o2 = o * o
    s0 = jnp.sum(jnp.where(lane_in_head0, o2, 0.0), axis=-1, keepdims=True)
    s1 = jnp.sum(jnp.where(lane_in_head0, 0.0, o2), axis=-1, keepdims=True)
    ms = jnp.where(lane_in_head0, s0, s1) * (1.0 / HEAD)
    return o * lax.rsqrt(ms + EPS) * g * out_scale


def _df_prompt_body(lam_ref, gs_ref, q_ref, k_ref, v_ref, o_ref, acc_ref, *stats, blk, scale, lam_init, shifted):
    qi = pl.program_id(2)
    lane = lax.broadcasted_iota(jnp.int32, (1, LANES), 1)
    q = q_ref[...] * (scale * LOG2_E)
    qm = [jnp.where((lane >= i * DF_QK) & (lane < (i + 1) * DF_QK), q, 0.0).astype(BF16) for i in range(4)]
    r = lax.broadcasted_iota(jnp.int32, (blk, blk), 0)
    c = lax.broadcasted_iota(jnp.int32, (blk, blk), 1)
    causal = c <= r
    if shifted:
        m_ref, l_ref = stats
        bufs = (None, None)
    else:
        bufs = stats

    def scores(j, buf):
        if shifted:
            return
        start = pl.multiple_of(j * blk, blk)
        k = k_ref[:, pl.ds(start, blk)].astype(BF16)
        for i in range(4):
            buf[i] = jnp.dot(qm[i], k, preferred_element_type=F32)

    def tile_raw(j, cur, nxt, diag):
        scores(jnp.maximum(j - 1, 0), nxt)
        start = pl.multiple_of(j * blk, blk)
        v = v_ref[:, pl.ds(start, blk)].astype(BF16)
        feat = lax.broadcasted_iota(jnp.int32, (LANES, 1), 0)
        one = jnp.ones(v.shape, BF16)
        vh = [jnp.where(feat < HEAD, v, one), jnp.where(feat < HEAD, one, v)]
        ss = [cur[i] for i in range(4)]
        ps = [jnp.exp2(s) for s in ss]
        if diag:
            ps = [jnp.where(causal, p, 0.0) for p in ps]
        pvs = [lax.dot_general(ps[i].astype(BF16), vh[i // 2], _NT, preferred_element_type=F32)
               for i in range(4)]
        for i in range(4):
            if diag:
                acc_ref[i] = pvs[i]
            else:
                acc_ref[i] += pvs[i]

    def tile_shifted(j, cur, nxt, diag):
        start = pl.multiple_of(j * blk, blk)
        k = k_ref[:, pl.ds(start, blk)].astype(BF16)
        v = v_ref[:, pl.ds(start, blk)].astype(BF16)
        for i in range(4):
            s = jnp.dot(qm[i], k, preferred_element_type=F32)
            if diag:
                s = jnp.where(causal, s, NEG_INF)
                m_new = jnp.max(s, axis=-1, keepdims=True)
                p = jnp.exp2(s - m_new)
                l_ref[i] = jnp.sum(p, axis=-1, keepdims=True)
                acc_ref[i] = lax.dot_general(p.astype(BF16), v, _NT, preferred_element_type=F32)
            else:
                m_old = m_ref[i]
                m_new = jnp.maximum(m_old, jnp.max(s, axis=-1, keepdims=True))
                a = jnp.exp2(m_old - m_new)
                p = jnp.exp2(s - m_new)
                l_ref[i] = a * l_ref[i] + jnp.sum(p, axis=-1, keepdims=True)
                acc_ref[i] = a * acc_ref[i] + lax.dot_general(p.astype(BF16), v, _NT,
                                                              preferred_element_type=F32)
            m_ref[i] = m_new

    _sweep_blocks(qi, scores, tile_shifted if shifted else tile_raw, bufs)
    lam = _diff_lambda(lam_ref, lam_init)
    if shifted:
        den = [l_ref[i] for i in range(4)]
    else:
        den = [pltpu.roll(acc_ref[i], HEAD, 1) for i in range(4)]
    o0 = acc_ref[0] / den[0] - lam * (acc_ref[1] / den[1])
    o1 = acc_ref[2] / den[2] - lam * (acc_ref[3] / den[3])
    o = jnp.where(lane < HEAD, o0, o1)
    o_ref[...] = _head_rms(o, gs_ref[...], lane < HEAD, 1.0 - lam_init)


def _df_prompt(lam_rows, gs2, q, k, v, n_seq, lam_init, shifted):
    T = q.shape[0]
    S = T // n_seq
    blk = min(ATT_BLOCK, S)
    nq = S // blk
    pairs = MIX // LANES
    qspec = pl.BlockSpec((blk, LANES), lambda b, p, i: (b * nq + i, p))
    kspec = pl.BlockSpec((None, None, LANES, S), lambda b, p, i: (k.shape[0] - 1, b, p, 0))
    stats = [pltpu.VMEM((4, blk, 1), F32)] * 2 if shifted else [pltpu.VMEM((4, blk, blk), F32)] * 2
    return pl.pallas_call(
        functools.partial(_df_prompt_body, blk=blk, scale=DF_QK ** -0.5, lam_init=lam_init, shifted=shifted),
        grid=(n_seq, pairs, nq),
        in_specs=[_full(lam_rows.shape), _full((1, LANES)), qspec, kspec, kspec],
        out_specs=qspec,
        out_shape=jax.ShapeDtypeStruct((T, MIX), F32),
        scratch_shapes=[pltpu.VMEM((4, blk, LANES), F32)] + stats,
        compiler_params=_params(("parallel", "parallel", "arbitrary")),
        name="df_prompt_shifted" if shifted else "df_prompt",
    )(lam_rows, gs2, q, k, v)


def _merge_body(x_ref, g_ref, oa_ref, ob_ref, oc_ref, od_ref, wg_ref, wb_ref, wo_ref, out_ref):
    x = x_ref[...]
    D = x.shape[1]
    h = _rms(x, g_ref[...]).astype(BF16)
    acc = None
    for i, o_ref in enumerate((oa_ref, ob_ref, oc_ref, od_ref)):
        gate = jnp.dot(h, wg_ref[:, i * D:(i + 1) * D], preferred_element_type=F32)
        branch = jnp.dot(o_ref[...].astype(BF16), wb_ref[i], preferred_element_type=F32)
        t = branch / (1.0 + jnp.exp(-gate))
        acc = t if acc is None else acc + t
    out_ref[...] = x + jnp.dot(acc.astype(BF16), wo_ref[...], preferred_element_type=F32)


def _merge(x, g, oa, ob, oc, od, wg, wb, wo):
    T, D = x.shape
    tm = min(ROW_TILE, T)
    row = pl.BlockSpec((tm, D), lambda i: (i, 0))
    mix = pl.BlockSpec((tm, MIX), lambda i: (i, 0))
    return pl.pallas_call(
        _merge_body,
        grid=(T // tm,),
        in_specs=[row, _full((1, D)), mix, mix, mix, mix, _full(wg.shape), _full(wb.shape), _full(wo.shape)],
        out_specs=row,
        out_shape=jax.ShapeDtypeStruct((T, D), F32),
        compiler_params=_params(("parallel",)),
        name="merge",
    )(x, g, oa, ob, oc, od, wg, wb, wo)


def _memkv_body(m_ref, wt_ref, g_ref, bd_ref, k_ref, v_ref):
    yt = lax.dot_general(wt_ref[...], m_ref[...].astype(BF16), _NT, preferred_element_type=F32)
    k = yt[:MIX]
    ms = jnp.dot(bd_ref[...], (k * k).astype(BF16), preferred_element_type=F32)
    k_ref[...] = k * lax.rsqrt(ms + EPS) * g_ref[...]
    v_ref[...] = yt[MIX:]


def _memkv(mem, wt, g_col, bd64, n_seq):
    T, D = mem.shape
    M = T // n_seq
    out = pl.BlockSpec((None, MIX, M), lambda b: (b, 0, 0))
    return pl.pallas_call(
        _memkv_body,
        grid=(n_seq,),
        in_specs=[pl.BlockSpec((M, D), lambda b: (b, 0)), _full(wt.shape), _full((MIX, 1)), _full((MIX, MIX))],
        out_specs=[out, out],
        out_shape=[jax.ShapeDtypeStruct((n_seq, MIX, M), F32)] * 2,
        compiler_params=_params(("parallel",)),
        name="mem_kv",
    )(mem, wt, g_col, bd64)


def _cross_query(x, g, wq, gq, bd):
    h = _rms(x, g).astype(BF16)
    q = jnp.dot(h, wq, preferred_element_type=F32)
    return _group_rms(q, gq, bd) * HEAD ** -0.5


def _cross_heads(q, mk, mv):
    lane = lax.broadcasted_iota(jnp.int32, (1, LANES), 1)
    outs = []
    for p in range(MIX // LANES):
        qp = q[:, p * LANES:(p + 1) * LANES]
        kp = mk[p * LANES:(p + 1) * LANES, :].astype(BF16)
        vp = mv[p * LANES:(p + 1) * LANES, :].astype(BF16)
        oh = []
        for h in range(2):
            in_head = (lane >= h * HEAD) & (lane < (h + 1) * HEAD)
            s = jnp.dot(jnp.where(in_head, qp, 0.0).astype(BF16), kp, preferred_element_type=F32)
            e = jnp.exp(s - jnp.max(s, axis=-1, keepdims=True))
            o = lax.dot_general(e.astype(BF16), vp, _NT, preferred_element_type=F32)
            oh.append(o / jnp.sum(e, axis=-1, keepdims=True))
        outs.append(jnp.where(lane < HEAD, oh[0], oh[1]))
    return outs


def _cross_prompt_body(x_ref, g_ref, wq_ref, gq_ref, bd_ref, mk_ref, mv_ref, wo_ref, out_ref):
    x = x_ref[...]
    q = _cross_query(x, g_ref[...], wq_ref[...], gq_ref[...], bd_ref[...])
    o = jnp.concatenate(_cross_heads(q, mk_ref[...], mv_ref[...]), axis=-1)
    out_ref[...] = x + jnp.dot(o.astype(BF16), wo_ref[...], preferred_element_type=F32)


def _cross_prompt(x, g, wq, gq, bd64, mk, mv, wo, n_seq):
    T, D = x.shape
    S = T // n_seq
    M = mk.shape[2]
    tm = min(ROW_TILE, S)
    nt = S // tm
    row = pl.BlockSpec((tm, D), lambda b, i: (b * nt + i, 0))
    mem = pl.BlockSpec((None, MIX, M), lambda b, i: (b, 0, 0))
    return pl.pallas_call(
        _cross_prompt_body,
        grid=(n_seq, nt),
        in_specs=[row, _full((1, D)), _full(wq.shape), _full((1, MIX)), _full((MIX, MIX)), mem, mem,
                  _full(wo.shape)],
        out_specs=row,
        out_shape=jax.ShapeDtypeStruct((T, D), F32),
        compiler_params=_params(("parallel", "parallel")),
        name="cross_prompt",
    )(x, g, wq, gq, bd64, mk, mv, wo)


def _cross_sample_body(x_ref, g_ref, wq_ref, gq_ref, bd_ref, mk_ref, mv_ref, wo_ref, out_ref, q_scr, acc_scr,
                       *, nb, ts):
    j = pl.program_id(0)

    @pl.when(j == 0)
    def _():
        q_scr[...] = _cross_query(x_ref[...], g_ref[...], wq_ref[...], gq_ref[...], bd_ref[...])
        acc_scr[...] = jnp.zeros(acc_scr.shape, F32)

    rows = lax.broadcasted_iota(jnp.int32, (nb * ts, 1), 0)
    mine = rows == j
    for t in range(1, ts):
        mine = mine | (rows == t * nb + j)
    outs = _cross_heads(q_scr[...], mk_ref[...], mv_ref[...])
    for p, o in enumerate(outs):
        sl = slice(p * LANES, (p + 1) * LANES)
        acc_scr[:, sl] = jnp.where(mine, o, acc_scr[:, sl])

    @pl.when(j == nb - 1)
    def _():
        out_ref[...] = x_ref[...] + jnp.dot(acc_scr[...].astype(BF16), wo_ref[...], preferred_element_type=F32)


def _cross_sample(x, g, wq, gq, bd64, cache_k, cache_v, layer, wo, nb, ts):
    T, D = x.shape
    M = cache_k.shape[3]
    mem = pl.BlockSpec((None, None, MIX, M), lambda j: (layer, j, 0, 0))
    return pl.pallas_call(
        functools.partial(_cross_sample_body, nb=nb, ts=ts),
        grid=(nb,),
        in_specs=[_full((T, D)), _full((1, D)), _full(wq.shape), _full((1, MIX)), _full((MIX, MIX)), mem, mem,
                  _full(wo.shape)],
        out_specs=_full((T, D)),
        out_shape=jax.ShapeDtypeStruct((T, D), F32),
        scratch_shapes=[pltpu.VMEM((T, MIX), F32), pltpu.VMEM((T, MIX), F32)],
        compiler_params=_params(("arbitrary",)),
        name="cross_sample",
    )(x, g, wq, gq, bd64, cache_k, cache_v, wo)


def _conv_gate(a0, a1, a2, b, cw):
    ac = cw[3:4] + cw[0:1] * a2 + cw[1:2] * a1 + cw[2:3] * a0
    return ac / (1.0 + jnp.exp(-ac)) * b


def _ffn_prompt_body(x_ref, xh_ref, g_ref, wup_ref, cw_ref, wdn_ref, out_ref, cst_ref, h_scr, a_scr, b_scr,
                     acc_scr, *, tm, F, tf, tiles_per_seq):
    halo = BF16_ROWS
    first = (pl.program_id(0) % tiles_per_seq) == 0
    x = x_ref[...]
    h_scr[0:halo, :] = _rms(xh_ref[...], g_ref[...]).astype(BF16)
    h_scr[halo:halo + tm, :] = _rms(x, g_ref[...]).astype(BF16)
    n = F // tf

    def up(j, slot):
        a = jnp.dot(h_scr[...], wup_ref[:, j * tf:(j + 1) * tf], preferred_element_type=F32)
        a_scr[slot] = a
        a_scr[slot, 0:halo, :] = jnp.where(first, 0.0, a[0:halo])
        b_scr[slot] = jnp.dot(h_scr[halo:halo + tm, :], wup_ref[:, F + j * tf:F + (j + 1) * tf],
                              preferred_element_type=F32)

    up(0, 0)
    for j in range(n):
        slot = j % 2
        cols = slice(j * tf, (j + 1) * tf)
        if j + 1 < n:
            up(j + 1, 1 - slot)
        act = _conv_gate(a_scr[slot, halo:halo + tm, :], a_scr[slot, halo - 1:halo - 1 + tm, :],
                         a_scr[slot, halo - 2:halo - 2 + tm, :], b_scr[slot], cw_ref[:, cols])
        contrib = jnp.dot(act.astype(BF16), wdn_ref[cols, :], preferred_element_type=F32)
        if j == 0:
            acc_scr[...] = contrib
        else:
            acc_scr[...] += contrib
        cst_ref[0, :, cols] = a_scr[slot, tm:tm + halo, :]
    out_ref[...] = x + acc_scr[...]


def _ffn_prompt(x, g, wup, cw, wdn, n_seq):
    T, D = x.shape
    F = wdn.shape[0]
    S = T // n_seq
    tm = min(ROW_TILE, S)
    nt = S // tm
    halo = BF16_ROWS
    tf = FFN_CHUNK if F % FFN_CHUNK == 0 else LANES
    row = pl.BlockSpec((tm, D), lambda i: (i, 0))
    prev = pl.BlockSpec((halo, D), lambda i: (jnp.maximum(i * (tm // halo) - 1, 0), 0))
    return pl.pallas_call(
        functools.partial(_ffn_prompt_body, tm=tm, F=F, tf=tf, tiles_per_seq=nt),
        grid=(T // tm,),
        in_specs=[row, prev, _full((1, D)), _full(wup.shape), _full(cw.shape), _full(wdn.shape)],
        out_specs=[row, pl.BlockSpec((1, halo, F), lambda i: (i // nt, 0, 0))],
        out_shape=[jax.ShapeDtypeStruct((T, D), F32), jax.ShapeDtypeStruct((n_seq, halo, F), F32)],
        scratch_shapes=[pltpu.VMEM((halo + tm, D), BF16), pltpu.VMEM((2, halo + tm, tf), F32),
                        pltpu.VMEM((2, tm, tf), F32), pltpu.VMEM((tm, D), F32)],
        compiler_params=_params(("arbitrary",)),
        name="ffn_prompt",
    )(x, x, g, wup, cw, wdn)


def _ffn_sample_body(x_ref, prev_ref, g_ref, wup_ref, cw_ref, wdn_ref, out_ref, cst_ref, *, nb, F, tf):
    x = x_ref[...]
    T = x.shape[0]
    h = _rms(x, g_ref[...]).astype(BF16)
    acc = None
    for j in range(F // tf):
        cols = slice(j * tf, (j + 1) * tf)
        a = jnp.dot(h, wup_ref[:, cols], preferred_element_type=F32)
        b = jnp.dot(h, wup_ref[:, F + j * tf:F + (j + 1) * tf], preferred_element_type=F32)
        prev = prev_ref[:, cols]
        a1 = jnp.concatenate([prev[nb:2 * nb], a[:T - nb]], axis=0)
        a2 = jnp.concatenate([prev, a[:T - 2 * nb]], axis=0)
        act = _conv_gate(a, a1, a2, b, cw_ref[:, cols])
        contrib = jnp.dot(act.astype(BF16), wdn_ref[cols, :], preferred_element_type=F32)
        acc = contrib if acc is None else acc + contrib
        cst_ref[:, cols] = a[T - 2 * nb:]
    out_ref[...] = x + acc


def _ffn_sample(x, prev, g, wup, cw, wdn, nb):
    T, D = x.shape
    F = wdn.shape[0]
    tf = FFN_CHUNK if F % FFN_CHUNK == 0 else LANES
    return pl.pallas_call(
        functools.partial(_ffn_sample_body, nb=nb, F=F, tf=tf),
        grid=(1,),
        in_specs=[_full((T, D)), _full(prev.shape), _full((1, D)), _full(wup.shape), _full(cw.shape),
                  _full(wdn.shape)],
        out_specs=[_full((T, D)), _full(prev.shape)],
        out_shape=[jax.ShapeDtypeStruct((T, D), F32), jax.ShapeDtypeStruct(prev.shape, F32)],
        compiler_params=_params(("arbitrary",)),
        name="ffn_sample",
    )(x, prev, g, wup, cw, wdn)


def _local_sample_body(xb_ref, st_ref, uc_ref, vn_ref, wp_ref, sp_ref, coef_ref, bias_ref, ob_ref, oc_ref,
                       *, nb, ts, n_past):
    wl = _lane_window()
    ext = jnp.concatenate([st_ref[...], xb_ref[...]], axis=0)
    base = POOL_STATE * nb
    x = xb_ref[...]
    acc = x
    for k in range(1, POOL_STATE + 1):
        acc = acc + jnp.where(k < wl, ext[base - k * nb:base - k * nb + ts * nb], 0.0)
    for t in range(ts):
        rows = slice(t * nb, (t + 1) * nb)
        cnt = jnp.minimum(n_past + t + 1, wl).astype(F32)
        d = acc[rows] / cnt - x[rows]
        y = jnp.dot(d.astype(BF16), wp_ref[...], preferred_element_type=F32)
        ob_ref[rows, :] = y * sp_ref[...]
        mixed = bias_ref[t:t + 1, :]
        for s in range(t + 1):
            mixed = mixed + coef_ref[t * ts + s:t * ts + s + 1, :] * vn_ref[s * nb:(s + 1) * nb, :]
        oc_ref[rows, :] = uc_ref[rows, :] * mixed


def _local_sample(xb, state, uc, vn, wp_bd, sp, coef, bias, nb, ts, n_past):
    T = xb.shape[0]
    return pl.pallas_call(
        functools.partial(_local_sample_body, nb=nb, ts=ts, n_past=n_past),
        grid=(1,),
        in_specs=[_full((T, MIX)), _full(state.shape), _full((T, MIX)), _full((T, MIX)), _full((MIX, MIX)),
                  _full((1, MIX)), _full(coef.shape), _full(bias.shape)],
        out_specs=[_full((T, MIX))] * 2,
        out_shape=[jax.ShapeDtypeStruct((T, MIX), F32)] * 2,
        compiler_params=_params(("arbitrary",)),
        name="local_sample",
    )(xb, state, uc, vn, wp_bd, sp, coef, bias)


Q_PAD = 8


def _slot_queries(q8, slot_width):
    lane = lax.broadcasted_iota(jnp.int32, (1, MIX), 1)
    slots = [jnp.where((lane >= i * slot_width) & (lane < (i + 1) * slot_width), q8, 0.0)
             for i in range(MIX // slot_width)]
    return jnp.concatenate(slots, axis=0).astype(BF16)


def _paged_fetch(pt_ref, cache_k, cache_v, kbuf, vbuf, sem, *, layer, G, reverse, nb, nj):
    slots = kbuf.shape[0]
    ahead = slots - 1
    step = pl.program_id(0) * nj + pl.program_id(1)

    def copies(t):
        bb, jj = t // nj, t % nj
        group = nj - 1 - jj if reverse else jj
        sl = t % slots
        out = []
        for g in range(G):
            pid = pt_ref[bb, group * G + g]
            out.append(pltpu.make_async_copy(cache_k.at[layer, pid], kbuf.at[sl, g], sem.at[sl, 0, g]))
            out.append(pltpu.make_async_copy(cache_v.at[layer, pid], vbuf.at[sl, g], sem.at[sl, 1, g]))
        return out

    @pl.when(step == 0)
    def _():
        for t in range(min(ahead, nb * nj)):
            for c in copies(t):
                c.start()

    @pl.when(step + ahead < nb * nj)
    def _():
        for c in copies(step + ahead):
            c.start()

    for c in copies(step):
        c.wait()
    slot = step % slots
    return [kbuf.at[slot, g] for g in range(G)], [vbuf.at[slot, g] for g in range(G)]


def _segment(page_refs, s, pages_per_seg):
    parts = [page_refs[s * pages_per_seg + g][...].astype(BF16) for g in range(pages_per_seg)]
    return parts[0] if pages_per_seg == 1 else jnp.concatenate(parts, axis=1)


def _sb_decode_body(pt_ref, q_ref, kn_ref, vn_ref, ck_ref, cv_ref, o_ref, acc_ref, car_ref, kbuf, vbuf, sem,
                    *, G, scale, layer, nb, nj):
    k_refs, v_refs = _paged_fetch(pt_ref, ck_ref, cv_ref, kbuf, vbuf, sem, layer=layer, G=G, reverse=True,
                                  nb=nb, nj=nj)
    j = pl.program_id(1)
    rows = (MIX // HEAD) * Q_PAD
    qbd = _slot_queries(q_ref[0] * (scale * LOG2_E), HEAD)

    def group(kbs, vbs, valid):
        pages = range(len(kbs))
        n = kbs[0].shape[1]
        upper = (lax.broadcasted_iota(jnp.int32, (n, n), 0) > lax.broadcasted_iota(jnp.int32, (n, n), 1))
        lss, lks = zip(*[_log2_sigmoids(jnp.dot(qbd, kb, preferred_element_type=F32)) for kb in kbs])
        if valid is not None:
            lks = [jnp.where(valid, lk, 0.0) for lk in lks]
        stacked = jnp.concatenate([lk.astype(BF16) for lk in lks], axis=0)
        bt_rows = jnp.dot(stacked, upper.astype(BF16), preferred_element_type=F32)
        bts = [bt_rows[p * rows:(p + 1) * rows] for p in pages]
        car = car_ref[...]
        cars = [None] * len(kbs)
        for p in reversed(pages):
            cars[p] = car
            car = car + (bts[p][:, 0:1] + lks[p][:, 0:1])
        pv = None
        for p in pages:
            w = jnp.exp2(lss[p] + bts[p] + cars[p])
            if valid is not None:
                w = jnp.where(valid, w, 0.0)
            d = lax.dot_general(w.astype(BF16), vbs[p], _NT, preferred_element_type=F32)
            pv = d if pv is None else pv + d
        acc_ref[...] += pv
        car_ref[...] = car

    @pl.when(j == 0)
    def _():
        acc_ref[...] = jnp.zeros(acc_ref.shape, F32)
        car_ref[...] = jnp.zeros(car_ref.shape, F32)
        n = kn_ref.shape[2]
        t = lax.broadcasted_iota(jnp.int32, (rows, n), 0) & (Q_PAD - 1)
        s = lax.broadcasted_iota(jnp.int32, (rows, n), 1)
        group([kn_ref[0].astype(BF16)], [vn_ref[0].astype(BF16)], s < t)

    group([ref[...].astype(BF16) for ref in k_refs], [ref[...].astype(BF16) for ref in v_refs], None)

    @pl.when(j == nj - 1)
    def _():
        lane = lax.broadcasted_iota(jnp.int32, (1, MIX), 1)
        o = jnp.zeros((Q_PAD, MIX), F32)
        for h in range(MIX // HEAD):
            in_head = (lane >= h * HEAD) & (lane < (h + 1) * HEAD)
            o = jnp.where(in_head, acc_ref[h * Q_PAD:(h + 1) * Q_PAD, :], o)
        o_ref[0] = o


def _df_decode_body(pt_ref, lam_ref, gs_ref, q_ref, kn_ref, vn_ref, ck_ref, cv_ref, o_ref, acc_ref, m_ref, l_ref,
                    kbuf, vbuf, sem, *, G, scale, lam_init, layer, nb, nj):
    k_refs, v_refs = _paged_fetch(pt_ref, ck_ref, cv_ref, kbuf, vbuf, sem, layer=layer, G=G, reverse=False,
                                  nb=nb, nj=nj)
    j = pl.program_id(1)
    slots = MIX // DF_QK
    rows = slots * Q_PAD
    qbd = _slot_queries(q_ref[0] * (scale * LOG2_E), DF_QK)
    page = k_refs[0].shape[1]
    seg = max(ATT_BLOCK, page)

    def block(kbs, vbs, valid):
        ss = [jnp.dot(qbd, kb, preferred_element_type=F32) for kb in kbs]
        if valid is not None:
            ss = [jnp.where(valid, s, NEG_INF) for s in ss]
        top = ss[0]
        for s in ss[1:]:
            top = jnp.maximum(top, s)
        m_old = m_ref[...]
        m_new = jnp.maximum(m_old, jnp.max(top, axis=-1, keepdims=True))
        a = jnp.exp2(m_old - m_new)
        ps = [jnp.exp2(s - m_new) for s in ss]
        psum = ps[0]
        for p in ps[1:]:
            psum = psum + p
        pv = None
        for p, vb in zip(ps, vbs):
            d = lax.dot_general(p.astype(BF16), vb, _NT, preferred_element_type=F32)
            pv = d if pv is None else pv + d
        l_ref[...] = a * l_ref[...] + jnp.sum(psum, axis=-1, keepdims=True)
        acc_ref[...] = a * acc_ref[...] + pv
        m_ref[...] = m_new

    @pl.when(j == 0)
    def _():
        acc_ref[...] = jnp.zeros(acc_ref.shape, F32)
        l_ref[...] = jnp.zeros(l_ref.shape, F32)
        m_ref[...] = jnp.full(m_ref.shape, NEG_INF, F32)
        n = kn_ref.shape[2]
        t = lax.broadcasted_iota(jnp.int32, (rows, n), 0) & (Q_PAD - 1)
        s = lax.broadcasted_iota(jnp.int32, (rows, n), 1)
        block([kn_ref[0].astype(BF16)], [vn_ref[0].astype(BF16)], s <= t)

    pps = seg // page
    nseg = G // pps
    block([_segment(k_refs, s, pps) for s in range(nseg)], [_segment(v_refs, s, pps) for s in range(nseg)], None)

    @pl.when(j == nj - 1)
    def _():
        lane = lax.broadcasted_iota(jnp.int32, (1, MIX), 1)
        lam = _diff_lambda(lam_ref, lam_init)
        norm = acc_ref[...] / l_ref[...]
        o = jnp.zeros((Q_PAD, MIX), F32)
        for h in range(MIX // HEAD):
            in_head = (lane >= h * HEAD) & (lane < (h + 1) * HEAD)
            r0 = (2 * h) * Q_PAD
            oh = norm[r0:r0 + Q_PAD, :] - lam * norm[r0 + Q_PAD:r0 + 2 * Q_PAD, :]
            o = jnp.where(in_head, oh, o)
        lane2 = lax.broadcasted_iota(jnp.int32, (1, LANES), 1)
        parts = [_head_rms(o[:, p * LANES:(p + 1) * LANES], gs_ref[...], lane2 < HEAD, 1.0 - lam_init)
                 for p in range(MIX // LANES)]
        o_ref[0] = jnp.concatenate(parts, axis=-1)


def _decode_specs(cache_k, n_pages, G):
    nj = n_pages // G
    page = cache_k.shape[3]
    tok = pl.BlockSpec((1, Q_PAD, MIX), lambda b, j, pt: (b, 0, 0))
    new = pl.BlockSpec((1, MIX, LANES), lambda b, j, pt: (b, 0, 0))
    hbm = pl.BlockSpec(memory_space=pl.ANY)
    fetch_scratch = [pltpu.VMEM((DECODE_SLOTS, G, MIX, page), F32), pltpu.VMEM((DECODE_SLOTS, G, MIX, page), F32),
                     pltpu.SemaphoreType.DMA((DECODE_SLOTS, 2, G))]
    return nj, tok, new, hbm, fetch_scratch


def _sb_decode(page_table, q, k_new, v_new, cache_k, cache_v, layer):
    nb, n_pages = page_table.shape
    G = PAGES_PER_STEP
    nj, tok, new, hbm, fetch_scratch = _decode_specs(cache_k, n_pages, G)
    rows = (MIX // HEAD) * Q_PAD
    grid_spec = pltpu.PrefetchScalarGridSpec(
        num_scalar_prefetch=1,
        grid=(nb, nj),
        in_specs=[tok, new, new, hbm, hbm],
        out_specs=tok,
        scratch_shapes=[pltpu.VMEM((rows, MIX), F32), pltpu.VMEM((rows, 1), F32)] + fetch_scratch,
    )
    return pl.pallas_call(
        functools.partial(_sb_decode_body, G=G, scale=HEAD ** -0.5, layer=layer, nb=nb, nj=nj),
        grid_spec=grid_spec,
        out_shape=jax.ShapeDtypeStruct((nb, Q_PAD, MIX), F32),
        compiler_params=_params(("arbitrary", "arbitrary")),
        name="sb_decode",
    )(page_table, q, k_new, v_new, cache_k, cache_v)


def _df_decode(page_table, lam_rows, gs2, q, k_new, v_new, cache_k, cache_v, layer, lam_init):
    nb, n_pages = page_table.shape
    G = PAGES_PER_STEP
    nj, tok, new, hbm, fetch_scratch = _decode_specs(cache_k, n_pages, G)
    rows = (MIX // DF_QK) * Q_PAD
    const = lambda shape: pl.BlockSpec(shape, lambda b, j, pt: (0,) * len(shape))
    grid_spec = pltpu.PrefetchScalarGridSpec(
        num_scalar_prefetch=1,
        grid=(nb, nj),
        in_specs=[const(lam_rows.shape), const((1, LANES)), tok, new, new, hbm, hbm],
        out_specs=tok,
        scratch_shapes=[pltpu.VMEM((rows, MIX), F32), pltpu.VMEM((rows, 1), F32), pltpu.VMEM((rows, 1), F32)]
        + fetch_scratch,
    )
    return pl.pallas_call(
        functools.partial(_df_decode_body, G=G, scale=DF_QK ** -0.5, lam_init=lam_init, layer=layer, nb=nb,
                          nj=nj),
        grid_spec=grid_spec,
        out_shape=jax.ShapeDtypeStruct((nb, Q_PAD, MIX), F32),
        compiler_params=_params(("arbitrary", "arbitrary")),
        name="df_decode",
    )(page_table, lam_rows, gs2, q, k_new, v_new, cache_k, cache_v)


def _row(v):
    return v.reshape(1, -1).astype(F32)


def _tile_row(v, width):
    return jnp.tile(v.astype(F32), width // v.shape[0]).reshape(1, width)


def _pad_rows(a, rows):
    return jnp.pad(a, ((0, rows - a.shape[0]),) + ((0, 0),) * (a.ndim - 1))


def _block_diag(w):
    G, c, d = w.shape
    eye = jnp.eye(G, dtype=w.dtype)
    return (eye[:, None, :, None] * w[:, :, None, :]).reshape(G * c, G * d)


def kernel(x_prompt, x_sample, mem_prompt, cache_sb_k, cache_sb_v, cache_df_k, cache_df_v, cache_mem_k, cache_mem_v, state_pool, state_ffn_conv, page_table, g_mix, w_in, w_branch, w_out, w_pool, s_pool, g_gm, w_spatial, b_spatial, g_qd, g_kd, lam_q1, lam_k1, lam_q2, lam_k2, g_subln, g_cross, w_cq, w_ckv, w_co, g_cq, g_ck, g_ffn, w_up, w_conv, b_conv, w_down):
    B, S, D = x_prompt.shape
    NB, TS, _ = x_sample.shape
    depth = w_in.shape[0]
    M = mem_prompt.shape[1]
    F = w_down.shape[1]
    page = cache_sb_k.shape[2]
    n_pages = page_table.shape[1]
    n_past = n_pages * page
    heads = MIX // HEAD
    assert TS >= CONV_W - 1 and TS <= Q_PAD and TS <= GM_CHUNK and n_pages % PAGES_PER_STEP == 0

    bd32 = _block_diag_mean(MIX, DF_QK)
    bd64 = _block_diag_mean(MIX, HEAD)
    feat_major = lambda c: c.transpose(0, 1, 3, 4, 2).reshape(c.shape[0], c.shape[1], MIX, c.shape[2])
    csb_k, csb_v, cdf_k, cdf_v = (feat_major(c) for c in (cache_sb_k, cache_sb_v, cache_df_k, cache_df_v))
    cmem_k, cmem_v = feat_major(cache_mem_k), feat_major(cache_mem_v)
    tok_major = lambda a: a.reshape(a.shape[0], heads, HEAD, a.shape[2]).transpose(0, 3, 1, 2)

    xp = x_prompt.reshape(B * S, D)
    xs = x_sample.transpose(1, 0, 2).reshape(TS * NB, D)
    mem = mem_prompt.reshape(B * M, D)

    to_batch = lambda a: a.reshape(TS, NB, MIX).transpose(1, 0, 2)
    to_time = lambda a: a[:, :TS].transpose(1, 0, 2).reshape(TS * NB, MIX)
    pad_tok = lambda a, n: jnp.pad(to_batch(a), ((0, 0), (0, n - TS), (0, 0)))
    new_page = lambda a: jnp.pad(to_batch(a).transpose(0, 2, 1), ((0, 0), (0, 0), (0, LANES - TS)))
    kv_cols = lambda w: jnp.concatenate([w[:, i * MIX:(i + 1) * MIX] for i in (1, 2, 7, 8)], axis=1)
    tok_cols = lambda w: jnp.concatenate([w[:, i * MIX:(i + 1) * MIX] for i in (0, 3, 4, 5, 6)], axis=1)

    outs = [[] for _ in range(15)]
    prompt_kv = ()
    for l in range(depth):
        lam_init = 0.8 - 0.6 * math.exp(-0.3 * l)
        w_l = w_in[l]
        wtok = tok_cols(w_l).astype(BF16)
        wkv = kv_cols(w_l).astype(BF16)
        wkv_t = wkv.T
        wg = w_l[:, 9 * MIX:].astype(BF16)
        wb = w_branch[l].astype(BF16)
        wo = w_out[l].astype(BF16)
        gq, gk = _tile_row(g_qd[l], MIX), _tile_row(g_kd[l], MIX)
        gk_col = gk.reshape(MIX, 1)
        ggm, gmix = _row(g_gm[l]), _row(g_mix[l])
        wp_bd = _block_diag(w_pool[l]).astype(BF16)
        sp = _row(s_pool[l])
        bs_full = jnp.repeat(b_spatial[l].T, MIX // GM_GROUPS, axis=1).astype(F32)
        lam_rows = _pad_rows(jnp.stack([lam_q1[l], lam_k1[l], lam_q2[l], lam_k2[l]]).astype(F32), 8)
        gs2 = _tile_row(g_subln[l], LANES)
        wcq = w_cq[l].astype(BF16)
        wckv_t = w_ckv[l].astype(BF16).T
        wco = w_co[l].astype(BF16)
        gcq, gck_col = _tile_row(g_cq[l], MIX), _tile_row(g_ck[l], MIX).reshape(MIX, 1)
        wup = w_up[l].astype(BF16)
        wdn = w_down[l].astype(BF16)
        cw = _pad_rows(jnp.concatenate([w_conv[l], b_conv[l][None]], axis=0).astype(F32), 8)

        qa, xb, uc, vn, qd, ka, va, kd, vd = _proj_in(xp, gmix, wtok, wkv_t, gq, gk_col, ggm, bd32, B, True,
                                                      prompt_kv)
        prompt_kv = (ka, va, kd, vd)
        oa = _sb_prompt(qa, ka, va, B)
        ob, oc = _local_prompt(xb, uc, vn, wp_bd, sp, w_spatial[l].astype(F32), bs_full, B)
        raw_exp_ok = (DF_QK ** 0.5 * 1.01 * jnp.max(jnp.abs(g_qd[l])) * jnp.max(jnp.abs(g_kd[l]))
                      <= DF_RAW_EXP_BOUND)
        od = lax.cond(raw_exp_ok,
                      functools.partial(_df_prompt, n_seq=B, lam_init=lam_init, shifted=False),
                      functools.partial(_df_prompt, n_seq=B, lam_init=lam_init, shifted=True),
                      lam_rows, gs2, qd, kd, vd)
        xp = _merge(xp, gmix, oa, ob, oc, od, wg, wb, wo)
        mk, mv = _memkv(mem, wckv_t, gck_col, bd64, B)
        xp = _cross_prompt(xp, _row(g_cross[l]), wcq, gcq, bd64, mk, mv, wco, B)
        xp, cst = _ffn_prompt(xp, _row(g_ffn[l]), wup, cw, wdn, B)
        outs[4].append(tok_major(mk))
        outs[5].append(tok_major(mv))
        outs[6].append(xb.reshape(B, S, MIX)[:, S - POOL_STATE:])
        outs[7].append(cst[:, -(CONV_W - 1):])

        qa, xb, uc, vn, qd, ka, va, kd, vd = _proj_in(xs, gmix, wtok, wkv, gq, gk, ggm, bd32, 1, False)
        oa = _sb_decode(page_table, pad_tok(qa, Q_PAD), new_page(ka), new_page(va), csb_k, csb_v, l)
        od = _df_decode(page_table, lam_rows, gs2, pad_tok(qd, Q_PAD), new_page(kd), new_page(vd),
                        cdf_k, cdf_v, l, lam_init)
        state_t = state_pool[l].astype(F32).transpose(1, 0, 2).reshape(POOL_STATE * NB, MIX)
        ws_t = jnp.where(jnp.tril(jnp.ones((TS, TS), bool)), w_spatial[l][:, :TS, :TS], 0.0)
        coef = _pad_rows(jnp.repeat(ws_t.transpose(1, 2, 0).reshape(TS * TS, GM_GROUPS),
                                    MIX // GM_GROUPS, axis=1).astype(F32), -(-TS * TS // 8) * 8)
        bias = _pad_rows(bs_full[:TS], 8)
        ob, oc = _local_sample(xb, state_t, uc, vn, wp_bd, sp, coef, bias, NB, TS, n_past)
        xs = _merge(xs, gmix, to_time(oa), ob, oc, to_time(od), wg, wb, wo)
        xs = _cross_sample(xs, _row(g_cross[l]), wcq, gcq, bd64, cmem_k, cmem_v, l, wco, NB, TS)
        prev_t = state_ffn_conv[l].astype(F32).transpose(1, 0, 2).reshape((CONV_W - 1) * NB, F)
        xs, cst_s = _ffn_sample(xs, prev_t, _row(g_ffn[l]), wup, cw, wdn, NB)
        heads_s = lambda a: to_batch(a).reshape(NB, TS, heads, HEAD)
        outs[8].append(heads_s(ka))
        outs[9].append(heads_s(va))
        outs[10].append(heads_s(kd))
        outs[11].append(heads_s(vd))
        pool_ext = jnp.concatenate([state_pool[l].astype(F32), to_batch(xb)], axis=1)
        outs[12].append(pool_ext[:, -POOL_STATE:])
        outs[13].append(cst_s.reshape(CONV_W - 1, NB, F).transpose(1, 0, 2))
        outs[14].append(to_batch(vn))

    y_prompt = xp.reshape(B, S, D)
    y_sample = xs.reshape(TS, NB, D).transpose(1, 0, 2)
    stacked_kv = [a.reshape(depth, B, heads, HEAD, S).transpose(0, 1, 4, 2, 3) for a in prompt_kv]
    return (y_prompt, y_sample) + tuple(stacked_kv) + tuple(jnp.stack(o) for o in outs[4:])
```

```python
import functools
import math

import numpy as np
import jax
import jax.numpy as jnp
from jax import lax
from jax.experimental import pallas as pl
from jax.experimental.pallas import tpu as pltpu

F32 = jnp.float32
BF16 = jnp.bfloat16

EPS = 1e-6
NEG_INF = -1e30
LOG2_E = math.log2(math.e)
MIX = 256
HEAD = 64
DF_QK = 32
POOL_WINDOWS = (2, 4, 8, 16)
POOL_STATE = max(POOL_WINDOWS) - 1
POOL_PAD = 16
GM_CHUNK = 128
GM_GROUPS = 4
CONV_W = 3
LANES = 128
BF16_ROWS = 16
ROW_TILE = 512
ATT_BLOCK = 256
FFN_CHUNK = 256
PAGES_PER_STEP = 16
DECODE_SLOTS = 3
DF_RAW_EXP_BOUND = 40.0
VMEM_LIMIT = 56 * 1024 * 1024

_NT = (((1,), (1,)), ((), ()))


def _params(sem):
    return pltpu.CompilerParams(dimension_semantics=sem, vmem_limit_bytes=VMEM_LIMIT)


def _rms(x, g):
    return x * lax.rsqrt(jnp.mean(x * x, axis=-1, keepdims=True) + EPS) * g


def _group_rms(v, g, bd):
    ms = jnp.dot((v * v).astype(BF16), bd, preferred_element_type=F32)
    return v * lax.rsqrt(ms + EPS) * g


def _block_diag_mean(width, group):
    i = np.arange(width)
    return jnp.asarray((i[:, None] // group == i[None, :] // group) / group, BF16)


def _full(shape):
    return pl.BlockSpec(shape, lambda *_: (0,) * len(shape))


def _lane_window():
    lane = lax.broadcasted_iota(jnp.int32, (1, MIX), 1)
    gdim = MIX // len(POOL_WINDOWS)
    wl = jnp.full((1, MIX), POOL_WINDOWS[-1], jnp.int32)
    for g in range(len(POOL_WINDOWS) - 2, -1, -1):
        wl = jnp.where(lane < (g + 1) * gdim, POOL_WINDOWS[g], wl)
    return wl


def _proj_in_body(x_ref, g_ref, wtok_ref, wkv_ref, gq_ref, gk_ref, ggm_ref, bd_ref, *rest, kv_major, n_prev):
    prev_refs, outs = rest[:len(rest) - 9], rest[len(rest) - 9:]
    qa_ref, xb_ref, uc_ref, vn_ref, qd_ref, ka_ref, va_ref, kd_ref, vd_ref = outs
    h = _rms(x_ref[...], g_ref[...]).astype(BF16)
    y = jnp.dot(h, wtok_ref[...], preferred_element_type=F32)
    part = lambda i: y[:, i * MIX:(i + 1) * MIX]
    qa_ref[...] = part(0)
    xb_ref[...] = part(1)
    uc_ref[...] = part(2)
    vn_ref[...] = _rms(part(3), ggm_ref[...])
    qd_ref[...] = _group_rms(part(4), gq_ref[...], bd_ref[...])
    if kv_major:
        yt = lax.dot_general(wkv_ref[...], h, _NT, preferred_element_type=F32)
        kd = yt[2 * MIX:3 * MIX]
        ms = jnp.dot(bd_ref[...], (kd * kd).astype(BF16), preferred_element_type=F32)
        new = (yt[0:MIX], yt[MIX:2 * MIX], kd * lax.rsqrt(ms + EPS) * gk_ref[...], yt[3 * MIX:4 * MIX])
        for i, out_ref in enumerate((ka_ref, va_ref, kd_ref, vd_ref)):
            if n_prev:
                out_ref[0:n_prev] = prev_refs[i][...]
            out_ref[n_prev] = new[i]
    else:
        y2 = jnp.dot(h, wkv_ref[...], preferred_element_type=F32)
        ka_ref[...] = y2[:, 0:MIX]
        va_ref[...] = y2[:, MIX:2 * MIX]
        kd_ref[...] = _group_rms(y2[:, 2 * MIX:3 * MIX], gk_ref[...], bd_ref[...])
        vd_ref[...] = y2[:, 3 * MIX:4 * MIX]


def _proj_in(x, g, wtok, wkv, gq, gk, ggm, bd32, n_seq, kv_major, prev=()):
    T, D = x.shape
    S = T // n_seq
    tm = min(ROW_TILE, S)
    nt = S // tm
    row = pl.BlockSpec((tm, D), lambda i: (i, 0))
    out = pl.BlockSpec((tm, MIX), lambda i: (i, 0))
    tok_shape = jax.ShapeDtypeStruct((T, MIX), F32)
    n_prev = prev[0].shape[0] if prev else 0
    slabs = lambda n: pl.BlockSpec((n, None, MIX, tm), lambda i: (0, i // nt, 0, i % nt))
    if kv_major:
        kv_spec = slabs(n_prev + 1)
        kv_shape = jax.ShapeDtypeStruct((n_prev + 1, n_seq, MIX, S), F32)
    else:
        kv_spec, kv_shape = out, tok_shape
    return pl.pallas_call(
        functools.partial(_proj_in_body, kv_major=kv_major, n_prev=n_prev),
        grid=(T // tm,),
        in_specs=[row, _full((1, D)), _full(wtok.shape), _full(wkv.shape), _full((1, MIX)), _full(gk.shape),
                  _full((1, MIX)), _full((MIX, MIX))] + [slabs(n_prev)] * len(prev),
        out_specs=[out] * 5 + [kv_spec] * 4,
        out_shape=[tok_shape] * 5 + [kv_shape] * 4,
        compiler_params=_params(("parallel",)),
        name="proj_in",
    )(x, g, wtok, wkv, gq, gk, ggm, bd32, *prev)


def _local_prompt_body(xb_ref, uc_ref, vn_ref, wp_ref, sp_ref, ws_ref, bs_ref, ob_ref, oc_ref, ext_ref):
    S = xb_ref.shape[0]
    ext_ref[0:POOL_PAD, :] = jnp.zeros((POOL_PAD, MIX), F32)
    ext_ref[POOL_PAD:POOL_PAD + S, :] = xb_ref[...]
    wl = _lane_window()
    lane = lax.broadcasted_iota(jnp.int32, (1, MIX), 1)
    R = min(256, S)
    for c in range(S // R):
        x = xb_ref[c * R:(c + 1) * R, :]
        acc = x
        for k in range(1, POOL_STATE + 1):
            lo = POOL_PAD + c * R - k
            acc = acc + jnp.where(k < wl, ext_ref[lo:lo + R, :], 0.0)
        pos = c * R + lax.broadcasted_iota(jnp.int32, (R, 1), 0)
        cnt = jnp.minimum(pos + 1, wl).astype(F32)
        d = acc / cnt - x
        y = jnp.dot(d.astype(BF16), wp_ref[...], preferred_element_type=F32)
        ob_ref[c * R:(c + 1) * R, :] = y * sp_ref[...]

    C = min(GM_CHUNK, S)
    r = lax.broadcasted_iota(jnp.int32, (C, C), 0)
    cidx = lax.broadcasted_iota(jnp.int32, (C, C), 1)
    gdim = MIX // GM_GROUPS
    ws = [jnp.where(r >= cidx, ws_ref[g][:C, :C], 0.0).astype(BF16) for g in range(GM_GROUPS)]
    bias = bs_ref[0:C, :]
    for n in range(S // C):
        vn = vn_ref[n * C:(n + 1) * C, :].astype(BF16)
        mixed = jnp.dot(ws[0], vn, preferred_element_type=F32)
        for g in range(1, GM_GROUPS):
            mixed = jnp.where(lane >= g * gdim, jnp.dot(ws[g], vn, preferred_element_type=F32), mixed)
        oc_ref[n * C:(n + 1) * C, :] = uc_ref[n * C:(n + 1) * C, :] * (mixed + bias)


def _local_prompt(xb, uc, vn, wp_bd, sp, ws, bs_full, n_seq):
    T = xb.shape[0]
    S = T // n_seq
    seq = pl.BlockSpec((S, MIX), lambda b: (b, 0))
    return pl.pallas_call(
        _local_prompt_body,
        grid=(n_seq,),
        in_specs=[seq, seq, seq, _full((MIX, MIX)), _full((1, MIX)), _full(ws.shape), _full(bs_full.shape)],
        out_specs=[seq, seq],
        out_shape=[jax.ShapeDtypeStruct((T, MIX), F32)] * 2,
        scratch_shapes=[pltpu.VMEM((POOL_PAD + S, MIX), F32)],
        compiler_params=_params(("parallel",)),
        name="local_prompt",
    )(xb, uc, vn, wp_bd, sp, ws, bs_full)


def _log2_sigmoids(z2):
    ls2 = jnp.minimum(z2, 0.0) - jnp.log2(1.0 + jnp.exp2(-jnp.abs(z2)))
    return ls2, ls2 - z2


def _sweep_blocks(qi, scores, tile, bufs):
    a, b = bufs
    scores(qi, a)
    tile(qi, a, b, True)

    def pair(p, carry):
        j = qi - 1 - 2 * p
        tile(j, b, a, False)
        tile(j - 1, a, b, False)
        return carry

    lax.fori_loop(0, qi // 2, pair, 0)

    @pl.when(qi % 2 == 1)
    def _():
        tile(0, b, a, False)


def _each_query_block(block_fn, q_ref, o_ref, blk):
    def step(qi, carry):
        rows = pl.ds(pl.multiple_of(qi * blk, blk), blk)
        block_fn(qi, q_ref.at[rows, :], o_ref.at[rows, :])
        return carry

    lax.fori_loop(0, q_ref.shape[0] // blk, step, 0)


def _sb_prompt_body(q_ref, k_ref, v_ref, o_ref, *scratch, blk, scale):
    block_fn = functools.partial(_sb_query_block, k_ref=k_ref, v_ref=v_ref, scratch=scratch, blk=blk, scale=scale)
    _each_query_block(block_fn, q_ref, o_ref, blk)


def _sb_query_block(qi, q_ref, o_ref, *, k_ref, v_ref, scratch, blk, scale):
    acc_ref, car_ref, za_ref, zb_ref = scratch
    lane = lax.broadcasted_iota(jnp.int32, (1, LANES), 1)
    q = q_ref[...] * (scale * LOG2_E)
    qh = [jnp.where(lane < HEAD, q, 0.0).astype(BF16), jnp.where(lane >= HEAD, q, 0.0).astype(BF16)]
    r = lax.broadcasted_iota(jnp.int32, (blk, blk), 0)
    c = lax.broadcasted_iota(jnp.int32, (blk, blk), 1)
    upper = (r > c).astype(BF16)
    causal = c < r

    heads = range(2)

    def scores(j, buf):
        start = pl.multiple_of(j * blk, blk)
        k = k_ref[:, pl.ds(start, blk)].astype(BF16)
        for h in heads:
            buf[h] = jnp.dot(qh[h], k, preferred_element_type=F32)

    def tile(j, cur, nxt, diag):
        scores(jnp.maximum(j - 1, 0), nxt)
        start = pl.multiple_of(j * blk, blk)
        v = v_ref[:, pl.ds(start, blk)].astype(BF16)
        zs = [cur[h] for h in heads]
        lss, lks = zip(*[_log2_sigmoids(z) for z in zs])
        if diag:
            lks = [jnp.where(causal, lk, 0.0) for lk in lks]
        bts = [jnp.dot(lks[h].astype(BF16), upper, preferred_element_type=F32) for h in heads]
        ws = [jnp.exp2(lss[h] + bts[h]) for h in heads]
        if diag:
            ws = [jnp.where(causal, w, 0.0) for w in ws]
        pvs = [lax.dot_general(ws[h].astype(BF16), v, _NT, preferred_element_type=F32) for h in heads]
        for h in heads:
            tot = bts[h][:, 0:1] + lks[h][:, 0:1]
            if diag:
                acc_ref[h] = pvs[h]
                car_ref[h] = tot
            else:
                acc_ref[h] += jnp.exp2(car_ref[h]) * pvs[h]
                car_ref[h] += tot

    _sweep_blocks(qi, scores, tile, (za_ref, zb_ref))
    o_ref[...] = jnp.where(lane < HEAD, acc_ref[0], acc_ref[1])


def _sb_prompt(q, k, v, n_seq):
    T = q.shape[0]
    S = T // n_seq
    blk = min(ATT_BLOCK, S)
    pairs = MIX // LANES
    qspec = pl.BlockSpec((S, LANES), lambda b, p: (b, p))
    kspec = pl.BlockSpec((None, None, LANES, S), lambda b, p: (k.shape[0] - 1, b, p, 0))
    return pl.pallas_call(
        functools.partial(_sb_prompt_body, blk=blk, scale=HEAD ** -0.5),
        grid=(n_seq, pairs),
        in_specs=[qspec, kspec, kspec],
        out_specs=qspec,
        out_shape=jax.ShapeDtypeStruct((T, MIX), F32),
        scratch_shapes=[pltpu.VMEM((2, blk, LANES), F32), pltpu.VMEM((2, blk, 1), F32),
                        pltpu.VMEM((2, blk, blk), F32), pltpu.VMEM((2, blk, blk), F32)],
        compiler_params=_params(("parallel", "parallel")),
        name="sb_prompt",
    )(q, k, v)


def _diff_lambda(lam_ref, lam_init):
    s1 = jnp.sum(lam_ref[0:1, :] * lam_ref[1:2, :], axis=-1, keepdims=True)
    s2 = jnp.sum(lam_ref[2:3, :] * lam_ref[3:4, :], axis=-1, keepdims=True)
    return jnp.exp(s1) - jnp.exp(s2) + lam_init


def _head_rms(o, g, lane_in_head0, out_scale):
    o2 = o * o
    s0 = jnp.sum(jnp.where(lane_in_head0, o2, 0.0), axis=-1, keepdims=True)
    s1 = jnp.sum(jnp.where(lane_in_head0, 0.0, o2), axis=-1, keepdims=True)
    ms = jnp.where(lane_in_head0, s0, s1) * (1.0 / HEAD)
    return o * lax.rsqrt(ms + EPS) * g * out_scale


def _df_prompt_body(lam_ref, gs_ref, q_ref, k_ref, v_ref, o_ref, *scratch, blk, scale, lam_init, shifted):
    block_fn = functools.partial(_df_query_block, lam_ref=lam_ref, gs_ref=gs_ref, k_ref=k_ref, v_ref=v_ref,
                                 scratch=scratch, blk=blk, scale=scale, lam_init=lam_init, shifted=shifted)
    _each_query_block(block_fn, q_ref, o_ref, blk)


def _df_query_block(qi, q_ref, o_ref, *, lam_ref, gs_ref, k_ref, v_ref, scratch, blk, scale, lam_init, shifted):
    acc_ref, *stats = scratch
    lane = lax.broadcasted_iota(jnp.int32, (1, LANES), 1)
    q = q_ref[...] * (scale * LOG2_E)
    qm = [jnp.where((lane >= i * DF_QK) & (lane < (i + 1) * DF_QK), q, 0.0).astype(BF16) for i in range(4)]
    r = lax.broadcasted_iota(jnp.int32, (blk, blk), 0)
    c = lax.broadcasted_iota(jnp.int32, (blk, blk), 1)
    causal = c <= r
    if shifted:
        m_ref, l_ref = stats
        bufs = (None, None)
    else:
        bufs = stats

    def scores(j, buf):
        if shifted:
            return
        start = pl.multiple_of(j * blk, blk)
        k = k_ref[:, pl.ds(start, blk)].astype(BF16)
        for i in range(4):
            buf[i] = jnp.dot(qm[i], k, preferred_element_type=F32)

    def tile_raw(j, cur, nxt, diag):
        scores(jnp.maximum(j - 1, 0), nxt)
        start = pl.multiple_of(j * blk, blk)
        v = v_ref[:, pl.ds(start, blk)].astype(BF16)
        feat = lax.broadcasted_iota(jnp.int32, (LANES, 1), 0)
        one = jnp.ones(v.shape, BF16)
        vh = [jnp.where(feat < HEAD, v, one), jnp.where(feat < HEAD, one, v)]
        ss = [cur[i] for i in range(4)]
        ps = [jnp.exp2(s) for s in ss]
        if diag:
            ps = [jnp.where(causal, p, 0.0) for p in ps]
        pvs = [lax.dot_general(ps[i].astype(BF16), vh[i // 2], _NT, preferred_element_type=F32)
               for i in range(4)]
        for i in range(4):
            if diag:
                acc_ref[i] = pvs[i]
            else:
                acc_ref[i] += pvs[i]

    def tile_shifted(j, cur, nxt, diag):
        start = pl.multiple_of(j * blk, blk)
        k = k_ref[:, pl.ds(start, blk)].astype(BF16)
        v = v_ref[:, pl.ds(start, blk)].astype(BF16)
        for i in range(4):
            s = jnp.dot(qm[i], k, preferred_element_type=F32)
            if diag:
                s = jnp.where(causal, s, NEG_INF)
                m_new = jnp.max(s, axis=-1, keepdims=True)
                p = jnp.exp2(s - m_new)
                l_ref[i] = jnp.sum(p, axis=-1, keepdims=True)
                acc_ref[i] = lax.dot_general(p.astype(BF16), v, _NT, preferred_element_type=F32)
            else:
                m_old = m_ref[i]
                m_new = jnp.maximum(m_old, jnp.max(s, axis=-1, keepdims=True))
                a = jnp.exp2(m_old - m_new)
                p = jnp.exp2(s - m_new)
                l_ref[i] = a * l_ref[i] + jnp.sum(p, axis=-1, keepdims=True)
                acc_ref[i] = a * acc_ref[i] + lax.dot_general(p.astype(BF16), v, _NT,
                                                              preferred_element_type=F32)
            m_ref[i] = m_new

    _sweep_blocks(qi, scores, tile_shifted if shifted else tile_raw, bufs)
    lam = _diff_lambda(lam_ref, lam_init)
    if shifted:
        den = [l_ref[i] for i in range(4)]
    else:
        den = [pltpu.roll(acc_ref[i], HEAD, 1) for i in range(4)]
    o0 = acc_ref[0] / den[0] - lam * (acc_ref[1] / den[1])
    o1 = acc_ref[2] / den[2] - lam * (acc_ref[3] / den[3])
    o = jnp.where(lane < HEAD, o0, o1)
    o_ref[...] = _head_rms(o, gs_ref[...], lane < HEAD, 1.0 - lam_init)


def _df_prompt(lam_rows, gs2, q, k, v, n_seq, lam_init, shifted):
    T = q.shape[0]
    S = T // n_seq
    blk = min(ATT_BLOCK, S)
    pairs = MIX // LANES
    qspec = pl.BlockSpec((S, LANES), lambda b, p: (b, p))
    kspec = pl.BlockSpec((None, None, LANES, S), lambda b, p: (k.shape[0] - 1, b, p, 0))
    stats = [pltpu.VMEM((4, blk, 1), F32)] * 2 if shifted else [pltpu.VMEM((4, blk, blk), F32)] * 2
    return pl.pallas_call(
        functools.partial(_df_prompt_body, blk=blk, scale=DF_QK ** -0.5, lam_init=lam_init, shifted=shifted),
        grid=(n_seq, pairs),
        in_specs=[_full(lam_rows.shape), _full((1, LANES)), qspec, kspec, kspec],
        out_specs=qspec,
        out_shape=jax.ShapeDtypeStruct((T, MIX), F32),
        scratch_shapes=[pltpu.VMEM((4, blk, LANES), F32)] + stats,
        compiler_params=_params(("parallel", "parallel")),
        name="df_prompt_shifted" if shifted else "df_prompt",
    )(lam_rows, gs2, q, k, v)


def _merge_body(x_ref, g_ref, oa_ref, ob_ref, oc_ref, od_ref, wg_ref, wb_ref, wo_ref, out_ref):
    x = x_ref[...]
    D = x.shape[1]
    h = _rms(x, g_ref[...]).astype(BF16)
    acc = None
    for i, o_ref in enumerate((oa_ref, ob_ref, oc_ref, od_ref)):
        gate = jnp.dot(h, wg_ref[:, i * D:(i + 1) * D], preferred_element_type=F32)
        branch = jnp.dot(o_ref[...].astype(BF16), wb_ref[i], preferred_element_type=F32)
        t = branch / (1.0 + jnp.exp(-gate))
        acc = t if acc is None else acc + t
    out_ref[...] = x + jnp.dot(acc.astype(BF16), wo_ref[...], preferred_element_type=F32)


def _merge(x, g, oa, ob, oc, od, wg, wb, wo):
    T, D = x.shape
    tm = min(ROW_TILE, T)
    row = pl.BlockSpec((tm, D), lambda i: (i, 0))
    mix = pl.BlockSpec((tm, MIX), lambda i: (i, 0))
    return pl.pallas_call(
        _merge_body,
        grid=(T // tm,),
        in_specs=[row, _full((1, D)), mix, mix, mix, mix, _full(wg.shape), _full(wb.shape), _full(wo.shape)],
        out_specs=row,
        out_shape=jax.ShapeDtypeStruct((T, D), F32),
        compiler_params=_params(("parallel",)),
        name="merge",
    )(x, g, oa, ob, oc, od, wg, wb, wo)


def _memkv_body(m_ref, wt_ref, g_ref, bd_ref, k_ref, v_ref):
    yt = lax.dot_general(wt_ref[...], m_ref[...].astype(BF16), _NT, preferred_element_type=F32)
    k = yt[:MIX]
    ms = jnp.dot(bd_ref[...], (k * k).astype(BF16), preferred_element_type=F32)
    k_ref[...] = k * lax.rsqrt(ms + EPS) * g_ref[...]
    v_ref[...] = yt[MIX:]


def _memkv(mem, wt, g_col, bd64, n_seq):
    T, D = mem.shape
    M = T // n_seq
    out = pl.BlockSpec((None, MIX, M), lambda b: (b, 0, 0))
    return pl.pallas_call(
        _memkv_body,
        grid=(n_seq,),
        in_specs=[pl.BlockSpec((M, D), lambda b: (b, 0)), _full(wt.shape), _full((MIX, 1)), _full((MIX, MIX))],
        out_specs=[out, out],
        out_shape=[jax.ShapeDtypeStruct((n_seq, MIX, M), F32)] * 2,
        compiler_params=_params(("parallel",)),
        name="mem_kv",
    )(mem, wt, g_col, bd64)


def _cross_query(x, g, wq, gq, bd):
    h = _rms(x, g).astype(BF16)
    q = jnp.dot(h, wq, preferred_element_type=F32)
    return _group_rms(q, gq, bd) * HEAD ** -0.5


def _cross_heads(q, mk, mv):
    lane = lax.broadcasted_iota(jnp.int32, (1, LANES), 1)
    outs = []
    for p in range(MIX // LANES):
        qp = q[:, p * LANES:(p + 1) * LANES]
        kp = mk[p * LANES:(p + 1) * LANES, :].astype(BF16)
        vp = mv[p * LANES:(p + 1) * LANES, :].astype(BF16)
        oh = []
        for h in range(2):
            in_head = (lane >= h * HEAD) & (lane < (h + 1) * HEAD)
            s = jnp.dot(jnp.where(in_head, qp, 0.0).astype(BF16), kp, preferred_element_type=F32)
            e = jnp.exp(s - jnp.max(s, axis=-1, keepdims=True))
            o = lax.dot_general(e.astype(BF16), vp, _NT, preferred_element_type=F32)
            oh.append(o / jnp.sum(e, axis=-1, keepdims=True))
        outs.append(jnp.where(lane < HEAD, oh[0], oh[1]))
    return outs


def _cross_prompt_body(x_ref, g_ref, wq_ref, gq_ref, bd_ref, mk_ref, mv_ref, wo_ref, out_ref):
    x = x_ref[...]
    q = _cross_query(x, g_ref[...], wq_ref[...], gq_ref[...], bd_ref[...])
    o = jnp.concatenate(_cross_heads(q, mk_ref[...], mv_ref[...]), axis=-1)
    out_ref[...] = x + jnp.dot(o.astype(BF16), wo_ref[...], preferred_element_type=F32)


def _cross_prompt(x, g, wq, gq, bd64, mk, mv, wo, n_seq):
    T, D = x.shape
    S = T // n_seq
    M = mk.shape[2]
    tm = min(ROW_TILE, S)
    nt = S // tm
    row = pl.BlockSpec((tm, D), lambda b, i: (b * nt + i, 0))
    mem = pl.BlockSpec((None, MIX, M), lambda b, i: (b, 0, 0))
    return pl.pallas_call(
        _cross_prompt_body,
        grid=(n_seq, nt),
        in_specs=[row, _full((1, D)), _full(wq.shape), _full((1, MIX)), _full((MIX, MIX)), mem, mem,
                  _full(wo.shape)],
        out_specs=row,
        out_shape=jax.ShapeDtypeStruct((T, D), F32),
        compiler_params=_params(("parallel", "parallel")),
        name="cross_prompt",
    )(x, g, wq, gq, bd64, mk, mv, wo)


def _cross_sample_body(x_ref, g_ref, wq_ref, gq_ref, bd_ref, mk_ref, mv_ref, wo_ref, out_ref, q_scr, acc_scr,
                       *, nb, ts):
    j = pl.program_id(0)

    @pl.when(j == 0)
    def _():
        q_scr[...] = _cross_query(x_ref[...], g_ref[...], wq_ref[...], gq_ref[...], bd_ref[...])
        acc_scr[...] = jnp.zeros(acc_scr.shape, F32)

    rows = lax.broadcasted_iota(jnp.int32, (nb * ts, 1), 0)
    mine = rows == j
    for t in range(1, ts):
        mine = mine | (rows == t * nb + j)
    outs = _cross_heads(q_scr[...], mk_ref[...], mv_ref[...])
    for p, o in enumerate(outs):
        sl = slice(p * LANES, (p + 1) * LANES)
        acc_scr[:, sl] = jnp.where(mine, o, acc_scr[:, sl])

    @pl.when(j == nb - 1)
    def _():
        out_ref[...] = x_ref[...] + jnp.dot(acc_scr[...].astype(BF16), wo_ref[...], preferred_element_type=F32)


def _cross_sample(x, g, wq, gq, bd64, cache_k, cache_v, layer, wo, nb, ts):
    T, D = x.shape
    M = cache_k.shape[3]
    mem = pl.BlockSpec((None, None, MIX, M), lambda j: (layer, j, 0, 0))
    return pl.pallas_call(
        functools.partial(_cross_sample_body, nb=nb, ts=ts),
        grid=(nb,),
        in_specs=[_full((T, D)), _full((1, D)), _full(wq.shape), _full((1, MIX)), _full((MIX, MIX)), mem, mem,
                  _full(wo.shape)],
        out_specs=_full((T, D)),
        out_shape=jax.ShapeDtypeStruct((T, D), F32),
        scratch_shapes=[pltpu.VMEM((T, MIX), F32), pltpu.VMEM((T, MIX), F32)],
        compiler_params=_params(("arbitrary",)),
        name="cross_sample",
    )(x, g, wq, gq, bd64, cache_k, cache_v, wo)


def _conv_gate(a0, a1, a2, b, cw):
    ac = cw[3:4] + cw[0:1] * a2 + cw[1:2] * a1 + cw[2:3] * a0
    return ac / (1.0 + jnp.exp(-ac)) * b


def _ffn_prompt_body(x_ref, xh_ref, g_ref, wup_ref, cw_ref, wdn_ref, out_ref, cst_ref, h_scr, a_scr, b_scr,
                     acc_scr, *, tm, F, tf, tiles_per_seq):
    halo = BF16_ROWS
    first = (pl.program_id(0) % tiles_per_seq) == 0
    x = x_ref[...]
    h_scr[0:halo, :] = _rms(xh_ref[...], g_ref[...]).astype(BF16)
    h_scr[halo:halo + tm, :] = _rms(x, g_ref[...]).astype(BF16)
    n = F // tf

    def up(j, slot):
        a = jnp.dot(h_scr[...], wup_ref[:, j * tf:(j + 1) * tf], preferred_element_type=F32)
        a_scr[slot] = a
        a_scr[slot, 0:halo, :] = jnp.where(first, 0.0, a[0:halo])
        b_scr[slot] = jnp.dot(h_scr[halo:halo + tm, :], wup_ref[:, F + j * tf:F + (j + 1) * tf],
                              preferred_element_type=F32)

    up(0, 0)
    for j in range(n):
        slot = j % 2
        cols = slice(j * tf, (j + 1) * tf)
        if j + 1 < n:
            up(j + 1, 1 - slot)
        act = _conv_gate(a_scr[slot, halo:halo + tm, :], a_scr[slot, halo - 1:halo - 1 + tm, :],
                         a_scr[slot, halo - 2:halo - 2 + tm, :], b_scr[slot], cw_ref[:, cols])
        contrib = jnp.dot(act.astype(BF16), wdn_ref[cols, :], preferred_element_type=F32)
        if j == 0:
            acc_scr[...] = contrib
        else:
            acc_scr[...] += contrib
        cst_ref[0, :, cols] = a_scr[slot, tm:tm + halo, :]
    out_ref[...] = x + acc_scr[...]


def _ffn_prompt(x, g, wup, cw, wdn, n_seq):
    T, D = x.shape
    F = wdn.shape[0]
    S = T // n_seq
    tm = min(ROW_TILE, S)
    nt = S // tm
    halo = BF16_ROWS
    tf = FFN_CHUNK if F % FFN_CHUNK == 0 else LANES
    row = pl.BlockSpec((tm, D), lambda i: (i, 0))
    prev = pl.BlockSpec((halo, D), lambda i: (jnp.maximum(i * (tm // halo) - 1, 0), 0))
    return pl.pallas_call(
        functools.partial(_ffn_prompt_body, tm=tm, F=F, tf=tf, tiles_per_seq=nt),
        grid=(T // tm,),
        in_specs=[row, prev, _full((1, D)), _full(wup.shape), _full(cw.shape), _full(wdn.shape)],
        out_specs=[row, pl.BlockSpec((1, halo, F), lambda i: (i // nt, 0, 0))],
        out_shape=[jax.ShapeDtypeStruct((T, D), F32), jax.ShapeDtypeStruct((n_seq, halo, F), F32)],
        scratch_shapes=[pltpu.VMEM((halo + tm, D), BF16), pltpu.VMEM((2, halo + tm, tf), F32),
                        pltpu.VMEM((2, tm, tf), F32), pltpu.VMEM((tm, D), F32)],
        compiler_params=_params(("arbitrary",)),
        name="ffn_prompt",
    )(x, x, g, wup, cw, wdn)


def _ffn_sample_body(x_ref, prev_ref, g_ref, wup_ref, cw_ref, wdn_ref, out_ref, cst_ref, *, nb, F, tf):
    x = x_ref[...]
    T = x.shape[0]
    h = _rms(x, g_ref[...]).astype(BF16)
    acc = None
    for j in range(F // tf):
        cols = slice(j * tf, (j + 1) * tf)
        a = jnp.dot(h, wup_ref[:, cols], preferred_element_type=F32)
        b = jnp.dot(h, wup_ref[:, F + j * tf:F + (j + 1) * tf], preferred_element_type=F32)
        prev = prev_ref[:, cols]
        a1 = jnp.concatenate([prev[nb:2 * nb], a[:T - nb]], axis=0)
        a2 = jnp.concatenate([prev, a[:T - 2 * nb]], axis=0)
        act = _conv_gate(a, a1, a2, b, cw_ref[:, cols])
        contrib = jnp.dot(act.astype(BF16), wdn_ref[cols, :], preferred_element_type=F32)
        acc = contrib if acc is None else acc + contrib
        cst_ref[:, cols] = a[T - 2 * nb:]
    out_ref[...] = x + acc


def _ffn_sample(x, prev, g, wup, cw, wdn, nb):
    T, D = x.shape
    F = wdn.shape[0]
    tf = FFN_CHUNK if F % FFN_CHUNK == 0 else LANES
    return pl.pallas_call(
        functools.partial(_ffn_sample_body, nb=nb, F=F, tf=tf),
        grid=(1,),
        in_specs=[_full((T, D)), _full(prev.shape), _full((1, D)), _full(wup.shape), _full(cw.shape),
                  _full(wdn.shape)],
        out_specs=[_full((T, D)), _full(prev.shape)],
        out_shape=[jax.ShapeDtypeStruct((T, D), F32), jax.ShapeDtypeStruct(prev.shape, F32)],
        compiler_params=_params(("arbitrary",)),
        name="ffn_sample",
    )(x, prev, g, wup, cw, wdn)


def _local_sample_body(xb_ref, st_ref, uc_ref, vn_ref, wp_ref, sp_ref, coef_ref, bias_ref, ob_ref, oc_ref,
                       *, nb, ts, n_past):
    wl = _lane_window()
    ext = jnp.concatenate([st_ref[...], xb_ref[...]], axis=0)
    base = POOL_STATE * nb
    x = xb_ref[...]
    acc = x
    for k in range(1, POOL_STATE + 1):
        acc = acc + jnp.where(k < wl, ext[base - k * nb:base - k * nb + ts * nb], 0.0)
    for t in range(ts):
        rows = slice(t * nb, (t + 1) * nb)
        cnt = jnp.minimum(n_past + t + 1, wl).astype(F32)
        d = acc[rows] / cnt - x[rows]
        y = jnp.dot(d.astype(BF16), wp_ref[...], preferred_element_type=F32)
        ob_ref[rows, :] = y * sp_ref[...]
        mixed = bias_ref[t:t + 1, :]
        for s in range(t + 1):
            mixed = mixed + coef_ref[t * ts + s:t * ts + s + 1, :] * vn_ref[s * nb:(s + 1) * nb, :]
        oc_ref[rows, :] = uc_ref[rows, :] * mixed


def _local_sample(xb, state, uc, vn, wp_bd, sp, coef, bias, nb, ts, n_past):
    T = xb.shape[0]
    return pl.pallas_call(
        functools.partial(_local_sample_body, nb=nb, ts=ts, n_past=n_past),
        grid=(1,),
        in_specs=[_full((T, MIX)), _full(state.shape), _full((T, MIX)), _full((T, MIX)), _full((MIX, MIX)),
                  _full((1, MIX)), _full(coef.shape), _full(bias.shape)],
        out_specs=[_full((T, MIX))] * 2,
        out_shape=[jax.ShapeDtypeStruct((T, MIX), F32)] * 2,
        compiler_params=_params(("arbitrary",)),
        name="local_sample",
    )(xb, state, uc, vn, wp_bd, sp, coef, bias)


Q_PAD = 8


def _slot_queries(q8, slot_width):
    lane = lax.broadcasted_iota(jnp.int32, (1, MIX), 1)
    slots = [jnp.where((lane >= i * slot_width) & (lane < (i + 1) * slot_width), q8, 0.0)
             for i in range(MIX // slot_width)]
    return jnp.concatenate(slots, axis=0).astype(BF16)


def _paged_fetch(pt_ref, cache_k, cache_v, kbuf, vbuf, sem, *, layer, G, reverse, nb, nj):
    slots = kbuf.shape[0]
    ahead = slots - 1
    step = pl.program_id(0) * nj + pl.program_id(1)

    def copies(t):
        bb, jj = t // nj, t % nj
        group = nj - 1 - jj if reverse else jj
        sl = t % slots
        out = []
        for g in range(G):
            pid = pt_ref[bb, group * G + g]
            out.append(pltpu.make_async_copy(cache_k.at[layer, pid], kbuf.at[sl, g], sem.at[sl, 0, g]))
            out.append(pltpu.make_async_copy(cache_v.at[layer, pid], vbuf.at[sl, g], sem.at[sl, 1, g]))
        return out

    @pl.when(step == 0)
    def _():
        for t in range(min(ahead, nb * nj)):
            for c in copies(t):
                c.start()

    @pl.when(step + ahead < nb * nj)
    def _():
        for c in copies(step + ahead):
            c.start()

    for c in copies(step):
        c.wait()
    slot = step % slots
    return [kbuf.at[slot, g] for g in range(G)], [vbuf.at[slot, g] for g in range(G)]


def _segment(page_refs, s, pages_per_seg):
    parts = [page_refs[s * pages_per_seg + g][...].astype(BF16) for g in range(pages_per_seg)]
    return parts[0] if pages_per_seg == 1 else jnp.concatenate(parts, axis=1)


def _sb_decode_body(pt_ref, q_ref, kn_ref, vn_ref, ck_ref, cv_ref, o_ref, acc_ref, car_ref, kbuf, vbuf, sem,
                    *, G, scale, layer, nb, nj):
    k_refs, v_refs = _paged_fetch(pt_ref, ck_ref, cv_ref, kbuf, vbuf, sem, layer=layer, G=G, reverse=True,
                                  nb=nb, nj=nj)
    j = pl.program_id(1)
    rows = (MIX // HEAD) * Q_PAD
    qbd = _slot_queries(q_ref[0] * (scale * LOG2_E), HEAD)

    def group(kbs, vbs, valid):
        pages = range(len(kbs))
        n = kbs[0].shape[1]
        upper = (lax.broadcasted_iota(jnp.int32, (n, n), 0) > lax.broadcasted_iota(jnp.int32, (n, n), 1))
        lss, lks = zip(*[_log2_sigmoids(jnp.dot(qbd, kb, preferred_element_type=F32)) for kb in kbs])
        if valid is not None:
            lks = [jnp.where(valid, lk, 0.0) for lk in lks]
        stacked = jnp.concatenate([lk.astype(BF16) for lk in lks], axis=0)
        bt_rows = jnp.dot(stacked, upper.astype(BF16), preferred_element_type=F32)
        bts = [bt_rows[p * rows:(p + 1) * rows] for p in pages]
        car = car_ref[...]
        cars = [None] * len(kbs)
        for p in reversed(pages):
            cars[p] = car
            car = car + (bts[p][:, 0:1] + lks[p][:, 0:1])
        pv = None
        for p in pages:
            w = jnp.exp2(lss[p] + bts[p] + cars[p])
            if valid is not None:
                w = jnp.where(valid, w, 0.0)
            d = lax.dot_general(w.astype(BF16), vbs[p], _NT, preferred_element_type=F32)
            pv = d if pv is None else pv + d
        acc_ref[...] += pv
        car_ref[...] = car

    @pl.when(j == 0)
    def _():
        acc_ref[...] = jnp.zeros(acc_ref.shape, F32)
        car_ref[...] = jnp.zeros(car_ref.shape, F32)
        n = kn_ref.shape[2]
        t = lax.broadcasted_iota(jnp.int32, (rows, n), 0) & (Q_PAD - 1)
        s = lax.broadcasted_iota(jnp.int32, (rows, n), 1)
        group([kn_ref[0].astype(BF16)], [vn_ref[0].astype(BF16)], s < t)

    group([ref[...].astype(BF16) for ref in k_refs], [ref[...].astype(BF16) for ref in v_refs], None)

    @pl.when(j == nj - 1)
    def _():
        lane = lax.broadcasted_iota(jnp.int32, (1, MIX), 1)
        o = jnp.zeros((Q_PAD, MIX), F32)
        for h in range(MIX // HEAD):
            in_head = (lane >= h * HEAD) & (lane < (h + 1) * HEAD)
            o = jnp.where(in_head, acc_ref[h * Q_PAD:(h + 1) * Q_PAD, :], o)
        o_ref[0] = o


def _df_decode_body(pt_ref, lam_ref, gs_ref, q_ref, kn_ref, vn_ref, ck_ref, cv_ref, o_ref, acc_ref, m_ref, l_ref,
                    kbuf, vbuf, sem, *, G, scale, lam_init, layer, nb, nj):
    k_refs, v_refs = _paged_fetch(pt_ref, ck_ref, cv_ref, kbuf, vbuf, sem, layer=layer, G=G, reverse=False,
                                  nb=nb, nj=nj)
    j = pl.program_id(1)
    slots = MIX // DF_QK
    rows = slots * Q_PAD
    qbd = _slot_queries(q_ref[0] * (scale * LOG2_E), DF_QK)
    page = k_refs[0].shape[1]
    seg = max(ATT_BLOCK, page)

    def block(kbs, vbs, valid):
        ss = [jnp.dot(qbd, kb, preferred_element_type=F32) for kb in kbs]
        if valid is not None:
            ss = [jnp.where(valid, s, NEG_INF) for s in ss]
        top = ss[0]
        for s in ss[1:]:
            top = jnp.maximum(top, s)
        m_old = m_ref[...]
        m_new = jnp.maximum(m_old, jnp.max(top, axis=-1, keepdims=True))
        a = jnp.exp2(m_old - m_new)
        ps = [jnp.exp2(s - m_new) for s in ss]
        psum = ps[0]
        for p in ps[1:]:
            psum = psum + p
        pv = None
        for p, vb in zip(ps, vbs):
            d = lax.dot_general(p.astype(BF16), vb, _NT, preferred_element_type=F32)
            pv = d if pv is None else pv + d
        l_ref[...] = a * l_ref[...] + jnp.sum(psum, axis=-1, keepdims=True)
        acc_ref[...] = a * acc_ref[...] + pv
        m_ref[...] = m_new

    @pl.when(j == 0)
    def _():
        acc_ref[...] = jnp.zeros(acc_ref.shape, F32)
        l_ref[...] = jnp.zeros(l_ref.shape, F32)
        m_ref[...] = jnp.full(m_ref.shape, NEG_INF, F32)
        n = kn_ref.shape[2]
        t = lax.broadcasted_iota(jnp.int32, (rows, n), 0) & (Q_PAD - 1)
        s = lax.broadcasted_iota(jnp.int32, (rows, n), 1)
        block([kn_ref[0].astype(BF16)], [vn_ref[0].astype(BF16)], s <= t)

    pps = seg // page
    nseg = G // pps
    block([_segment(k_refs, s, pps) for s in range(nseg)], [_segment(v_refs, s, pps) for s in range(nseg)], None)

    @pl.when(j == nj - 1)
    def _():
        lane = lax.broadcasted_iota(jnp.int32, (1, MIX), 1)
        lam = _diff_lambda(lam_ref, lam_init)
        norm = acc_ref[...] / l_ref[...]
        o = jnp.zeros((Q_PAD, MIX), F32)
        for h in range(MIX // HEAD):
            in_head = (lane >= h * HEAD) & (lane < (h + 1) * HEAD)
            r0 = (2 * h) * Q_PAD
            oh = norm[r0:r0 + Q_PAD, :] - lam * norm[r0 + Q_PAD:r0 + 2 * Q_PAD, :]
            o = jnp.where(in_head, oh, o)
        lane2 = lax.broadcasted_iota(jnp.int32, (1, LANES), 1)
        parts = [_head_rms(o[:, p * LANES:(p + 1) * LANES], gs_ref[...], lane2 < HEAD, 1.0 - lam_init)
                 for p in range(MIX // LANES)]
        o_ref[0] = jnp.concatenate(parts, axis=-1)


def _decode_specs(cache_k, n_pages, G):
    nj = n_pages // G
    page = cache_k.shape[3]
    tok = pl.BlockSpec((1, Q_PAD, MIX), lambda b, j, pt: (b, 0, 0))
    new = pl.BlockSpec((1, MIX, LANES), lambda b, j, pt: (b, 0, 0))
    hbm = pl.BlockSpec(memory_space=pl.ANY)
    fetch_scratch = [pltpu.VMEM((DECODE_SLOTS, G, MIX, page), F32), pltpu.VMEM((DECODE_SLOTS, G, MIX, page), F32),
                     pltpu.SemaphoreType.DMA((DECODE_SLOTS, 2, G))]
    return nj, tok, new, hbm, fetch_scratch


def _sb_decode(page_table, q, k_new, v_new, cache_k, cache_v, layer):
    nb, n_pages = page_table.shape
    G = PAGES_PER_STEP
    nj, tok, new, hbm, fetch_scratch = _decode_specs(cache_k, n_pages, G)
    rows = (MIX // HEAD) * Q_PAD
    grid_spec = pltpu.PrefetchScalarGridSpec(
        num_scalar_prefetch=1,
        grid=(nb, nj),
        in_specs=[tok, new, new, hbm, hbm],
        out_specs=tok,
        scratch_shapes=[pltpu.VMEM((rows, MIX), F32), pltpu.VMEM((rows, 1), F32)] + fetch_scratch,
    )
    return pl.pallas_call(
        functools.partial(_sb_decode_body, G=G, scale=HEAD ** -0.5, layer=layer, nb=nb, nj=nj),
        grid_spec=grid_spec,
        out_shape=jax.ShapeDtypeStruct((nb, Q_PAD, MIX), F32),
        compiler_params=_params(("arbitrary", "arbitrary")),
        name="sb_decode",
    )(page_table, q, k_new, v_new, cache_k, cache_v)


def _df_decode(page_table, lam_rows, gs2, q, k_new, v_new, cache_k, cache_v, layer, lam_init):
    nb, n_pages = page_table.shape
    G = PAGES_PER_STEP
    nj, tok, new, hbm, fetch_scratch = _decode_specs(cache_k, n_pages, G)
    rows = (MIX // DF_QK) * Q_PAD
    const = lambda shape: pl.BlockSpec(shape, lambda b, j, pt: (0,) * len(shape))
    grid_spec = pltpu.PrefetchScalarGridSpec(
        num_scalar_prefetch=1,
        grid=(nb, nj),
        in_specs=[const(lam_rows.shape), const((1, LANES)), tok, new, new, hbm, hbm],
        out_specs=tok,
        scratch_shapes=[pltpu.VMEM((rows, MIX), F32), pltpu.VMEM((rows, 1), F32), pltpu.VMEM((rows, 1), F32)]
        + fetch_scratch,
    )
    return pl.pallas_call(
        functools.partial(_df_decode_body, G=G, scale=DF_QK ** -0.5, lam_init=lam_init, layer=layer, nb=nb,
                          nj=nj),
        grid_spec=grid_spec,
        out_shape=jax.ShapeDtypeStruct((nb, Q_PAD, MIX), F32),
        compiler_params=_params(("arbitrary", "arbitrary")),
        name="df_decode",
    )(page_table, lam_rows, gs2, q, k_new, v_new, cache_k, cache_v)


def _row(v):
    return v.reshape(1, -1).astype(F32)


def _tile_row(v, width):
    return jnp.tile(v.astype(F32), width // v.shape[0]).reshape(1, width)


def _pad_rows(a, rows):
    return jnp.pad(a, ((0, rows - a.shape[0]),) + ((0, 0),) * (a.ndim - 1))


def _block_diag(w):
    G, c, d = w.shape
    eye = jnp.eye(G, dtype=w.dtype)
    return (eye[:, None, :, None] * w[:, :, None, :]).reshape(G * c, G * d)


def kernel(x_prompt, x_sample, mem_prompt, cache_sb_k, cache_sb_v, cache_df_k, cache_df_v, cache_mem_k, cache_mem_v, state_pool, state_ffn_conv, page_table, g_mix, w_in, w_branch, w_out, w_pool, s_pool, g_gm, w_spatial, b_spatial, g_qd, g_kd, lam_q1, lam_k1, lam_q2, lam_k2, g_subln, g_cross, w_cq, w_ckv, w_co, g_cq, g_ck, g_ffn, w_up, w_conv, b_conv, w_down):
    B, S, D = x_prompt.shape
    NB, TS, _ = x_sample.shape
    depth = w_in.shape[0]
    M = mem_prompt.shape[1]
    F = w_down.shape[1]
    page = cache_sb_k.shape[2]
    n_pages = page_table.shape[1]
    n_past = n_pages * page
    heads = MIX // HEAD
    assert TS >= CONV_W - 1 and TS <= Q_PAD and TS <= GM_CHUNK and n_pages % PAGES_PER_STEP == 0

    bd32 = _block_diag_mean(MIX, DF_QK)
    bd64 = _block_diag_mean(MIX, HEAD)
    feat_major = lambda c: c.transpose(0, 1, 3, 4, 2).reshape(c.shape[0], c.shape[1], MIX, c.shape[2])
    csb_k, csb_v, cdf_k, cdf_v = (feat_major(c) for c in (cache_sb_k, cache_sb_v, cache_df_k, cache_df_v))
    cmem_k, cmem_v = feat_major(cache_mem_k), feat_major(cache_mem_v)
    tok_major = lambda a: a.reshape(a.shape[0], heads, HEAD, a.shape[2]).transpose(0, 3, 1, 2)

    xp = x_prompt.reshape(B * S, D)
    xs = x_sample.transpose(1, 0, 2).reshape(TS * NB, D)
    mem = mem_prompt.reshape(B * M, D)

    to_batch = lambda a: a.reshape(TS, NB, MIX).transpose(1, 0, 2)
    to_time = lambda a: a[:, :TS].transpose(1, 0, 2).reshape(TS * NB, MIX)
    pad_tok = lambda a, n: jnp.pad(to_batch(a), ((0, 0), (0, n - TS), (0, 0)))
    new_page = lambda a: jnp.pad(to_batch(a).transpose(0, 2, 1), ((0, 0), (0, 0), (0, LANES - TS)))
    kv_cols = lambda w: jnp.concatenate([w[:, i * MIX:(i + 1) * MIX] for i in (1, 2, 7, 8)], axis=1)
    tok_cols = lambda w: jnp.concatenate([w[:, i * MIX:(i + 1) * MIX] for i in (0, 3, 4, 5, 6)], axis=1)

    outs = [[] for _ in range(15)]
    prompt_kv = ()
    for l in range(depth):
        lam_init = 0.8 - 0.6 * math.exp(-0.3 * l)
        w_l = w_in[l]
        wtok = tok_cols(w_l).astype(BF16)
        wkv = kv_cols(w_l).astype(BF16)
        wkv_t = wkv.T
        wg = w_l[:, 9 * MIX:].astype(BF16)
        wb = w_branch[l].astype(BF16)
        wo = w_out[l].astype(BF16)
        gq, gk = _tile_row(g_qd[l], MIX), _tile_row(g_kd[l], MIX)
        gk_col = gk.reshape(MIX, 1)
        ggm, gmix = _row(g_gm[l]), _row(g_mix[l])
        wp_bd = _block_diag(w_pool[l]).astype(BF16)
        sp = _row(s_pool[l])
        bs_full = jnp.repeat(b_spatial[l].T, MIX // GM_GROUPS, axis=1).astype(F32)
        lam_rows = _pad_rows(jnp.stack([lam_q1[l], lam_k1[l], lam_q2[l], lam_k2[l]]).astype(F32), 8)
        gs2 = _tile_row(g_subln[l], LANES)
        wcq = w_cq[l].astype(BF16)
        wckv_t = w_ckv[l].astype(BF16).T
        wco = w_co[l].astype(BF16)
        gcq, gck_col = _tile_row(g_cq[l], MIX), _tile_row(g_ck[l], MIX).reshape(MIX, 1)
        wup = w_up[l].astype(BF16)
        wdn = w_down[l].astype(BF16)
        cw = _pad_rows(jnp.concatenate([w_conv[l], b_conv[l][None]], axis=0).astype(F32), 8)

        qa, xb, uc, vn, qd, ka, va, kd, vd = _proj_in(xp, gmix, wtok, wkv_t, gq, gk_col, ggm, bd32, B, True,
                                                      prompt_kv)
        prompt_kv = (ka, va, kd, vd)
        oa = _sb_prompt(qa, ka, va, B)
        ob, oc = _local_prompt(xb, uc, vn, wp_bd, sp, w_spatial[l].astype(F32), bs_full, B)
        raw_exp_ok = (DF_QK ** 0.5 * 1.01 * jnp.max(jnp.abs(g_qd[l])) * jnp.max(jnp.abs(g_kd[l]))
                      <= DF_RAW_EXP_BOUND)
        od = lax.cond(raw_exp_ok,
                      functools.partial(_df_prompt, n_seq=B, lam_init=lam_init, shifted=False),
                      functools.partial(_df_prompt, n_seq=B, lam_init=lam_init, shifted=True),
                      lam_rows, gs2, qd, kd, vd)
        xp = _merge(xp, gmix, oa, ob, oc, od, wg, wb, wo)
        mk, mv = _memkv(mem, wckv_t, gck_col, bd64, B)
        xp = _cross_prompt(xp, _row(g_cross[l]), wcq, gcq, bd64, mk, mv, wco, B)
        xp, cst = _ffn_prompt(xp, _row(g_ffn[l]), wup, cw, wdn, B)
        outs[4].append(tok_major(mk))
        outs[5].append(tok_major(mv))
        outs[6].append(xb.reshape(B, S, MIX)[:, S - POOL_STATE:])
        outs[7].append(cst[:, -(CONV_W - 1):])

        qa, xb, uc, vn, qd, ka, va, kd, vd = _proj_in(xs, gmix, wtok, wkv, gq, gk, ggm, bd32, 1, False)
        oa = _sb_decode(page_table, pad_tok(qa, Q_PAD), new_page(ka), new_page(va), csb_k, csb_v, l)
        od = _df_decode(page_table, lam_rows, gs2, pad_tok(qd, Q_PAD), new_page(kd), new_page(vd),
                        cdf_k, cdf_v, l, lam_init)
        state_t = state_pool[l].astype(F32).transpose(1, 0, 2).reshape(POOL_STATE * NB, MIX)
        ws_t = jnp.where(jnp.tril(jnp.ones((TS, TS), bool)), w_spatial[l][:, :TS, :TS], 0.0)
        coef = _pad_rows(jnp.repeat(ws_t.transpose(1, 2, 0).reshape(TS * TS, GM_GROUPS),
                                    MIX // GM_GROUPS, axis=1).astype(F32), -(-TS * TS // 8) * 8)
        bias = _pad_rows(bs_full[:TS], 8)
        ob, oc = _local_sample(xb, state_t, uc, vn, wp_bd, sp, coef, bias, NB, TS, n_past)
        xs = _merge(xs, gmix, to_time(oa), ob, oc, to_time(od), wg, wb, wo)
        xs = _cross_sample(xs, _row(g_cross[l]), wcq, gcq, bd64, cmem_k, cmem_v, l, wco, NB, TS)
        prev_t = state_ffn_conv[l].astype(F32).transpose(1, 0, 2).reshape((CONV_W - 1) * NB, F)
        xs, cst_s = _ffn_sample(xs, prev_t, _row(g_ffn[l]), wup, cw, wdn, NB)
        heads_s = lambda a: to_batch(a).reshape(NB, TS, heads, HEAD)
        outs[8].append(heads_s(ka))
        outs[9].append(heads_s(va))
        outs[10].append(heads_s(kd))
        outs[11].append(heads_s(vd))
        pool_ext = jnp.concatenate([state_pool[l].astype(F32), to_batch(xb)], axis=1)
        outs[12].append(pool_ext[:, -POOL_STATE:])
        outs[13].append(cst_s.reshape(CONV_W - 1, NB, F).transpose(1, 0, 2))
        outs[14].append(to_batch(vn))

    y_prompt = xp.reshape(B, S, D)
    y_sample = xs.reshape(TS, NB, D).transpose(1, 0, 2)
    stacked_kv = [a.reshape(depth, B, heads, HEAD, S).transpose(0, 1, 4, 2, 3) for a in prompt_kv]
    return (y_prompt, y_sample) + tuple(stacked_kv) + tuple(jnp.stack(o) for o in outs[4:])
```
